```python
import functools
import jax, jax.numpy as jnp
from jax import lax
import numpy as np

D_MODEL = 1024
BATCH = 8
SEQ = 2048
DEPTH = 1
DEC_BATCH = 128
DEC_SEQ = 8
PAST_LEN = 16384
PAGE_SIZE = 128

FOX_HEADS = 8
FOX_HD = 64
FOX_W = FOX_HEADS * FOX_HD
MLA_HEADS = 8
MLA_NOPE = 64
MLA_ROPE = 32
MLA_VD = 64
MLA_Q_RANK = 256
MLA_KV_RANK = 128
MLA_W = MLA_HEADS * MLA_VD
N_EXPERTS = 64
N_GROUPS = 8
TOPK_GROUPS = 4
TOP_K = 8
EXPERT_FF = 256
SHARED_FF = 256
ROUTED_SCALE = 2.5
Q_BLOCK = 128
ROPE_BASE = 10000.0
RMS_EPS = 1e-6
FOX_SCALE = FOX_HD ** -0.5
MLA_SCALE = (MLA_NOPE + MLA_ROPE) ** -0.5
D_IN = 3 * FOX_W + FOX_HEADS + MLA_Q_RANK + MLA_KV_RANK + MLA_ROPE + 2 * D_MODEL

kernel_name = 'fox_mla_gated_moe_decode_step'


def _in_splits():
    sizes = (FOX_W, FOX_W, FOX_W, FOX_HEADS, MLA_Q_RANK, MLA_KV_RANK, MLA_ROPE, D_MODEL, D_MODEL)
    return [int(v) for v in np.cumsum(sizes)[:-1]]


def _rms_norm(x, g):
    xf = x.astype(jnp.float32)
    y = xf * lax.rsqrt(jnp.mean(xf * xf, axis=-1, keepdims=True) + RMS_EPS)
    return (y * g.astype(jnp.float32)).astype(x.dtype)


def _apply_rope(x, pos):
    half = x.shape[-1] // 2
    inv_freq = ROPE_BASE ** (-jnp.arange(half, dtype=jnp.float32) / half)
    ang = pos.astype(jnp.float32)[:, None] * inv_freq[None, :]
    cos = jnp.cos(ang)[None, :, None, :]
    sin = jnp.sin(ang)[None, :, None, :]
    xf = x.astype(jnp.float32)
    x1, x2 = xf[..., :half], xf[..., half:]
    return jnp.concatenate([x1 * cos - x2 * sin, x1 * sin + x2 * cos], axis=-1).astype(x.dtype)


def _online_step(m, l, acc, s, v, spec):
    m_new = jnp.maximum(m, jnp.max(s, axis=-1))
    corr = jnp.exp(m - m_new)
    p = jnp.exp(s - m_new[..., None])
    l_new = l * corr + jnp.sum(p, axis=-1)
    acc_new = acc * corr[..., None] + jnp.einsum(spec, p, v)
    return (m_new, l_new, acc_new)


def _prompt_attention(qa, ka, va, logf, q_lat, q_pe, ckv, kpe):
    B, S = qa.shape[0], qa.shape[1]
    n_blocks = S // Q_BLOCK
    f32 = jnp.float32
    cum = jnp.cumsum(logf, axis=1).transpose(0, 2, 1)
    key_pos = jnp.arange(S)
    va32 = va.astype(f32)
    ckv32 = ckv.astype(f32)

    def block(i):
        start = i * Q_BLOCK

        def sl(t, axis=1):
            return lax.dynamic_slice_in_dim(t, start, Q_BLOCK, axis=axis)

        causal = key_pos[None, :] <= (start + jnp.arange(Q_BLOCK))[:, None]
        s_a = jnp.einsum('bqhd,bkhd->bhqk', sl(qa), ka, preferred_element_type=f32) * FOX_SCALE
        s_a = s_a + sl(cum, 2)[..., :, None] - cum[..., None, :]
        p_a = jax.nn.softmax(jnp.where(causal, s_a, -jnp.inf), axis=-1)
        o_a = jnp.einsum('bhqk,bkhd->bqhd', p_a, va32)
        s_b = (jnp.einsum('bqhc,bkc->bhqk', sl(q_lat), ckv, preferred_element_type=f32)
               + jnp.einsum('bqhr,bkr->bhqk', sl(q_pe), kpe, preferred_element_type=f32)) * MLA_SCALE
        p_b = jax.nn.softmax(jnp.where(causal, s_b, -jnp.inf), axis=-1)
        o_lat = jnp.einsum('bhqk,bkc->bqhc', p_b, ckv32)
        return o_a, o_lat

    o_a, o_lat = lax.map(block, jnp.arange(n_blocks))
    o_a = jnp.moveaxis(o_a, 0, 1).reshape(B, S, FOX_HEADS, FOX_HD)
    o_lat = jnp.moveaxis(o_lat, 0, 1).reshape(B, S, MLA_HEADS, MLA_KV_RANK)
    return o_a, o_lat


def _sample_attention(qa, ka, va, logf, q_lat, q_pe, ckv, kpe, *, layer, cache_fox_k, cache_fox_v,
                      cache_fox_logf, cache_mla_ckv, cache_mla_kpe, page_table):
    DB, T = qa.shape[0], qa.shape[1]
    f32 = jnp.float32
    cn = jnp.cumsum(logf, axis=1).transpose(0, 2, 1)
    causal = jnp.tril(jnp.ones((T, T), dtype=bool))

    init_a = (jnp.full((DB, FOX_HEADS, T), -jnp.inf, f32), jnp.zeros((DB, FOX_HEADS, T), f32),
              jnp.zeros((DB, FOX_HEADS, T, FOX_HD), f32))
    init_b = (jnp.full((DB, MLA_HEADS, T), -jnp.inf, f32), jnp.zeros((DB, MLA_HEADS, T), f32),
              jnp.zeros((DB, MLA_HEADS, T, MLA_KV_RANK), f32))
    s_a = jnp.einsum('bthd,bkhd->bhtk', qa, ka, preferred_element_type=f32) * FOX_SCALE
    s_a = s_a + cn[..., :, None] - cn[..., None, :]
    st_a = _online_step(*init_a, jnp.where(causal, s_a, -jnp.inf), va, 'bhtk,bkhd->bhtd')
    s_b = (jnp.einsum('bthc,bkc->bhtk', q_lat, ckv, preferred_element_type=f32)
           + jnp.einsum('bthr,bkr->bhtk', q_pe, kpe, preferred_element_type=f32)) * MLA_SCALE
    st_b = _online_step(*init_b, jnp.where(causal, s_b, -jnp.inf), ckv, 'bhtk,bkc->bhtc')

    def page_step(carry, pt):
        st_a, st_b, suffix = carry
        k_pg = cache_fox_k[layer, pt]
        v_pg = cache_fox_v[layer, pt]
        lf = cache_fox_logf[layer, pt].astype(f32)
        incl = lax.cumsum(lf, axis=1, reverse=True)
        excl = jnp.concatenate([incl[:, 1:], jnp.zeros_like(incl[:, :1])], axis=1)
        r = (suffix[:, None, :] + excl).transpose(0, 2, 1)
        s_pa = jnp.einsum('bthd,bphd->bhtp', qa, k_pg, preferred_element_type=f32) * FOX_SCALE
        s_pa = s_pa + cn[..., :, None] + r[:, :, None, :]
        st_a = _online_step(*st_a, s_pa, v_pg, 'bhtp,bphd->bhtd')
        c_pg = cache_mla_ckv[layer, pt]
        kpe_pg = cache_mla_kpe[layer, pt]
        s_pb = (jnp.einsum('bthc,bpc->bhtp', q_lat, c_pg, preferred_element_type=f32)
                + jnp.einsum('bthr,bpr->bhtp', q_pe, kpe_pg, preferred_element_type=f32)) * MLA_SCALE
        st_b = _online_step(*st_b, s_pb, c_pg, 'bhtp,bpc->bhtc')
        return (st_a, st_b, suffix + incl[:, 0]), None

    init = (st_a, st_b, jnp.zeros((DB, FOX_HEADS), f32))
    (st_a, st_b, _), _ = lax.scan(page_step, init, page_table.T, reverse=True)
    o_a = (st_a[2] / st_a[1][..., None]).transpose(0, 2, 1, 3)
    o_lat = (st_b[2] / st_b[1][..., None]).transpose(0, 2, 1, 3)
    return o_a, o_lat


def _moe(h, w):
    f32 = jnp.float32
    e_per_g = N_EXPERTS // N_GROUPS

    def per_seq(hs):
        logits = jnp.einsum('sd,de->se', hs, w['w_router'], preferred_element_type=f32)
        score = jax.nn.sigmoid(logits)
        biased = score + w['router_bias'].astype(f32)
        grouped = biased.reshape(hs.shape[0], N_GROUPS, e_per_g)
        group_score = jnp.sum(lax.top_k(grouped, 2)[0], axis=-1)
        _, top_groups = lax.top_k(group_score, TOPK_GROUPS)
        group_mask = jnp.sum(jax.nn.one_hot(top_groups, N_GROUPS, dtype=f32), axis=-2)
        expert_mask = jnp.repeat(group_mask, e_per_g, axis=-1)
        _, top_e = lax.top_k(jnp.where(expert_mask > 0, biased, -jnp.inf), TOP_K)
        w_sel = jnp.take_along_axis(score, top_e, axis=-1)
        w_sel = w_sel / jnp.sum(w_sel, axis=-1, keepdims=True) * ROUTED_SCALE
        gates = jnp.sum(jax.nn.one_hot(top_e, N_EXPERTS, dtype=f32) * w_sel[..., None], axis=-2)
        hg = jnp.einsum('sd,edf->sef', hs, w['w_exp_gate'])
        hu = jnp.einsum('sd,edf->sef', hs, w['w_exp_up'])
        act = jax.nn.silu(hg) * hu * gates[..., None]
        routed = jnp.einsum('sef,efd->sd', act, w['w_exp_down'])
        shared = (jax.nn.silu(hs @ w['w_sh_gate']) * (hs @ w['w_sh_up'])) @ w['w_sh_down']
        return (routed + shared).astype(hs.dtype)

    return lax.map(per_seq, h)


def _layer(x, pos, attend, w):
    B, S = x.shape[0], x.shape[1]
    f32 = jnp.float32
    h = _rms_norm(x, w['g_attn'])
    z = jnp.einsum('bsd,de->bse', h, w['w_in'])
    qa, ka, va, fa, cq, ckv, kr, gate_a, gate_b = jnp.split(z, _in_splits(), axis=-1)
    qa = qa.reshape(B, S, FOX_HEADS, FOX_HD)
    ka = ka.reshape(B, S, FOX_HEADS, FOX_HD)
    va = va.reshape(B, S, FOX_HEADS, FOX_HD)
    logf = jax.nn.log_sigmoid(fa.astype(f32) + w['b_forget'].astype(f32))
    cq = _rms_norm(cq, w['g_q_lat'])
    q = jnp.einsum('bsr,rhe->bshe', cq, w['w_uq'])
    q_pe = _apply_rope(q[..., MLA_NOPE:], pos)
    q_lat = jnp.einsum('bshn,chn->bshc', q[..., :MLA_NOPE], w['w_uk'])
    ckv = _rms_norm(ckv, w['g_kv_lat'])
    kpe = _apply_rope(kr[:, :, None, :], pos)[:, :, 0, :]
    o_a, o_lat = attend(qa, ka, va, logf, q_lat, q_pe, ckv, kpe)
    o_b = jnp.einsum('bshc,chv->bshv', o_lat, w['w_uv'])
    branch_a = jnp.einsum('bse,ed->bsd', o_a.reshape(B, S, FOX_W).astype(x.dtype), w['w_branch_a'])
    branch_b = jnp.einsum('bse,ed->bsd', o_b.reshape(B, S, MLA_W).astype(x.dtype), w['w_branch_b'])
    merged = jax.nn.sigmoid(gate_a) * branch_a + jax.nn.sigmoid(gate_b) * branch_b
    x = x + jnp.einsum('bsd,de->bse', merged, w['w_out'])
    x = x + _moe(_rms_norm(x, w['g_ffn']), w)
    return x, (ka, va, logf, ckv, kpe)


def setup_inputs(seed: int = 0) -> dict:
    key = jax.random.key(seed)
    k = jax.random.split(key, 32)
    f32 = jnp.float32
    n_pages = PAST_LEN // PAGE_SIZE
    n_used = DEC_BATCH * n_pages
    n_pool = n_used + n_used // 4 + 1

    def normal(kk, shape, scale=1.0):
        return jax.random.normal(kk, shape, f32) * scale

    def gain(kk, shape):
        return 1.0 + 0.05 * jax.random.normal(kk, shape, f32)

    return {
        'x_prompt': normal(k[0], (BATCH, SEQ, D_MODEL)),
        'x_sample': normal(k[1], (DEC_BATCH, DEC_SEQ, D_MODEL)),
        'cache_fox_k': normal(k[2], (DEPTH, n_pool, PAGE_SIZE, FOX_HEADS, FOX_HD)),
        'cache_fox_v': normal(k[3], (DEPTH, n_pool, PAGE_SIZE, FOX_HEADS, FOX_HD)),
        'cache_fox_logf': jax.nn.log_sigmoid(3.0 + normal(k[4], (DEPTH, n_pool, PAGE_SIZE, FOX_HEADS))),
        'cache_mla_ckv': normal(k[5], (DEPTH, n_pool, PAGE_SIZE, MLA_KV_RANK)),
        'cache_mla_kpe': normal(k[6], (DEPTH, n_pool, PAGE_SIZE, MLA_ROPE)),
        'page_table': jax.random.permutation(k[7], n_pool)[:n_used].reshape(DEC_BATCH, n_pages).astype(jnp.int32),
        'g_attn': gain(k[8], (DEPTH, D_MODEL)),
        'w_in': normal(k[9], (DEPTH, D_MODEL, D_IN), D_MODEL ** -0.5),
        'b_forget': jax.random.uniform(k[10], (DEPTH, FOX_HEADS), f32, 1.0, 5.0),
        'g_q_lat': gain(k[11], (DEPTH, MLA_Q_RANK)),
        'w_uq': normal(k[12], (DEPTH, MLA_Q_RANK, MLA_HEADS, MLA_NOPE + MLA_ROPE), MLA_Q_RANK ** -0.5),
        'g_kv_lat': gain(k[13], (DEPTH, MLA_KV_RANK)),
        'w_uk': normal(k[14], (DEPTH, MLA_KV_RANK, MLA_HEADS, MLA_NOPE), MLA_KV_RANK ** -0.5),
        'w_uv': normal(k[15], (DEPTH, MLA_KV_RANK, MLA_HEADS, MLA_VD), MLA_KV_RANK ** -0.5),
        'w_branch_a': normal(k[16], (DEPTH, FOX_W, D_MODEL), FOX_W ** -0.5),
        'w_branch_b': normal(k[17], (DEPTH, MLA_W, D_MODEL), MLA_W ** -0.5),
        'w_out': normal(k[18], (DEPTH, D_MODEL, D_MODEL), D_MODEL ** -0.5),
        'g_ffn': gain(k[19], (DEPTH, D_MODEL)),
        'w_router': normal(k[20], (DEPTH, D_MODEL, N_EXPERTS), D_MODEL ** -0.5),
        'router_bias': normal(k[21], (DEPTH, N_EXPERTS), 0.01),
        'w_exp_gate': normal(k[22], (DEPTH, N_EXPERTS, D_MODEL, EXPERT_FF), D_MODEL ** -0.5),
        'w_exp_up': normal(k[23], (DEPTH, N_EXPERTS, D_MODEL, EXPERT_FF), D_MODEL ** -0.5),
        'w_exp_down': normal(k[24], (DEPTH, N_EXPERTS, EXPERT_FF, D_MODEL), EXPERT_FF ** -0.5),
        'w_sh_gate': normal(k[25], (DEPTH, D_MODEL, SHARED_FF), D_MODEL ** -0.5),
        'w_sh_up': normal(k[26], (DEPTH, D_MODEL, SHARED_FF), D_MODEL ** -0.5),
        'w_sh_down': normal(k[27], (DEPTH, SHARED_FF, D_MODEL), SHARED_FF ** -0.5),
        'g_final': gain(k[28], (D_MODEL,)),
    }


def reference(x_prompt, x_sample, cache_fox_k, cache_fox_v, cache_fox_logf, cache_mla_ckv, cache_mla_kpe,
              page_table, g_attn, w_in, b_forget, g_q_lat, w_uq, g_kv_lat, w_uk, w_uv, w_branch_a,
              w_branch_b, w_out, g_ffn, w_router, router_bias, w_exp_gate, w_exp_up, w_exp_down,
              w_sh_gate, w_sh_up, w_sh_down, g_final):
    past_len = page_table.shape[1] * cache_fox_k.shape[2]
    pos_p = jnp.arange(x_prompt.shape[1], dtype=jnp.int32)
    pos_s = past_len + jnp.arange(x_sample.shape[1], dtype=jnp.int32)
    yp, ys = x_prompt, x_sample
    st_p, st_s = [], []
    for l in range(DEPTH):
        w = {
            'g_attn': g_attn[l], 'w_in': w_in[l], 'b_forget': b_forget[l], 'g_q_lat': g_q_lat[l],
            'w_uq': w_uq[l], 'g_kv_lat': g_kv_lat[l], 'w_uk': w_uk[l], 'w_uv': w_uv[l],
            'w_branch_a': w_branch_a[l], 'w_branch_b': w_branch_b[l], 'w_out': w_out[l], 'g_ffn': g_ffn[l],
            'w_router': w_router[l], 'router_bias': router_bias[l], 'w_exp_gate': w_exp_gate[l],
            'w_exp_up': w_exp_up[l], 'w_exp_down': w_exp_down[l], 'w_sh_gate': w_sh_gate[l],
            'w_sh_up': w_sh_up[l], 'w_sh_down': w_sh_down[l],
        }
        yp, sp = _layer(yp, pos_p, _prompt_attention, w)
        sample_attend = functools.partial(
            _sample_attention, layer=l, cache_fox_k=cache_fox_k, cache_fox_v=cache_fox_v,
            cache_fox_logf=cache_fox_logf, cache_mla_ckv=cache_mla_ckv, cache_mla_kpe=cache_mla_kpe,
            page_table=page_table)
        ys, ss = _layer(ys, pos_s, sample_attend, w)
        st_p.append(sp)
        st_s.append(ss)
    y_prompt = _rms_norm(yp, g_final)
    y_sample = _rms_norm(ys, g_final)
    fox_k_p = jnp.stack([s[0] for s in st_p])
    fox_v_p = jnp.stack([s[1] for s in st_p])
    fox_logf_p = jnp.stack([s[2] for s in st_p])
    mla_ckv_p = jnp.stack([s[3] for s in st_p])
    mla_kpe_p = jnp.stack([s[4] for s in st_p])
    fox_k_s = jnp.stack([s[0] for s in st_s])
    fox_v_s = jnp.stack([s[1] for s in st_s])
    fox_logf_s = jnp.stack([s[2] for s in st_s])
    mla_ckv_s = jnp.stack([s[3] for s in st_s])
    mla_kpe_s = jnp.stack([s[4] for s in st_s])
    return (y_prompt, y_sample, fox_k_p, fox_v_p, fox_logf_p, mla_ckv_p, mla_kpe_p,
            fox_k_s, fox_v_s, fox_logf_s, mla_ckv_s, mla_kpe_s)
```

```python
import functools

import jax
import jax.numpy as jnp
import numpy as np
from jax import lax
from jax.experimental import pallas as pl
from jax.experimental.pallas import tpu as pltpu

F32 = jnp.float32
BF16 = jnp.bfloat16

RMS_EPS = 1e-6
ROPE_BASE = 10000.0
ROUTED_SCALE = 2.5
N_GROUPS = 8
TOPK_GROUPS = 4
TOP_K = 8

LANES = 128
SUBLANES = 8
VMEM_LIMIT = 56 * 1024 * 1024

_C_Q, _C_K, _C_V, _C_CQ, _C_CKV, _C_MISC, _C_GA, _C_GB, _C_END = 0, 512, 1024, 1536, 1792, 1920, 2304, 3328, 4352
AUG = 3


def _nt(a, b):
    return lax.dot_general(a, b, (((1,), (1,)), ((), ())), preferred_element_type=F32)


def _mm(a, b):
    return jnp.dot(a, b, preferred_element_type=F32)


def _rms(x, g):
    return x * lax.rsqrt(jnp.mean(x * x, axis=-1, keepdims=True) + RMS_EPS) * g


def _sigmoid(z):
    return 1.0 / (1.0 + jnp.exp(-z))


def _split3(x):
    hi = x.astype(BF16).astype(F32)
    r1 = x - hi
    mid = r1.astype(BF16).astype(F32)
    lo = r1 - mid
    return hi, mid, lo


def _const_spec(shape):
    nd = len(shape)
    return pl.BlockSpec(shape, lambda *_: (0,) * nd, pipeline_mode=pl.Buffered(1))


def _proj_kernel(x_ref, g_ref, w_ref, b_ref, gq_ref, wuq_ref, gkv_ref, wuk_ref, tri_ref, cos_ref, sin_ref,
                 k32_ref, v32_ref, logf_ref, cum_ref, ckv32_ref, kpe32_ref, qf_ref, kf_ref, vb_ref,
                 qcat_ref, kcat_ref, sga_ref, sgb_ref, carry_ref, *, tiles_per_seq, fox_scale, mla_scale):
    i = pl.program_id(0)
    tm = x_ref.shape[0]
    n_heads = qf_ref.shape[0]
    hb = _rms(x_ref[...], g_ref[...]).astype(BF16)

    def proj(a, b):
        return _mm(hb, w_ref[:, a:b])

    cos = cos_ref[...]
    sin = sin_ref[...]
    lane = lax.broadcasted_iota(jnp.int32, (tm, LANES), 1)

    zm = proj(_C_MISC, _C_GA)
    z = zm[:, :LANES] + b_ref[...]
    logf = jnp.where(lane < n_heads, jnp.minimum(z, 0.0) - jnp.log1p(jnp.exp(-jnp.abs(z))), 0.0)
    logf_ref[...] = logf[:, :n_heads]
    tri = tri_ref[...]
    hi, mid, lo = _split3(logf)
    cum = _mm(tri, hi.astype(BF16)) + _mm(tri, mid.astype(BF16)) + _mm(tri, lo.astype(BF16))
    if tiles_per_seq > 1:
        @pl.when(i % tiles_per_seq == 0)
        def _():
            carry_ref[...] = jnp.zeros_like(carry_ref)
        cum = cum + carry_ref[0:1, :]
        carry_ref[...] = jnp.broadcast_to(cum[tm - 1:tm, :], carry_ref.shape)
    cum_ref[...] = cum[:, :n_heads]

    kpe = zm[:, LANES:2 * LANES] * cos[:, :LANES] + zm[:, 2 * LANES:] * sin[:, :LANES]
    kpe32_ref[...] = kpe[:, :kpe32_ref.shape[1]]
    kcat_ref[:, LANES:] = kpe.astype(BF16)

    zq = proj(_C_Q, _C_K)
    zk = proj(_C_K, _C_V)
    k32_ref[...] = zk
    for h in range(n_heads):
        even = h % 2 == 0
        o = LANES // 2 if even else 0
        data = (lane < LANES // 2) if even else (lane >= LANES // 2)
        c_hi, c_mid, c_lo = _split3(jnp.broadcast_to(cum[:, h:h + 1], (tm, LANES)))
        ones_q = (lane >= o + AUG) & (lane < o + 2 * AUG)
        ones_k = (lane >= o) & (lane < o + AUG)
        aug_q = jnp.where(lane == o, c_hi, jnp.where(lane == o + 1, c_mid, jnp.where(lane == o + 2, c_lo,
                          jnp.where(ones_q, 1.0, 0.0))))
        aug_k = jnp.where(lane == o + AUG, -c_hi, jnp.where(lane == o + AUG + 1, -c_mid,
                          jnp.where(lane == o + AUG + 2, -c_lo, jnp.where(ones_k, 1.0, 0.0))))
        pair = slice((h // 2) * LANES, (h // 2 + 1) * LANES)
        qf_ref[h] = jnp.where(data, zq[:, pair] * fox_scale, aug_q).astype(qf_ref.dtype)
        kf_ref[h] = jnp.where(data, zk[:, pair], aug_k).astype(kf_ref.dtype)

    zv = proj(_C_V, _C_CQ)
    v32_ref[...] = zv
    vb_ref[...] = zv.astype(vb_ref.dtype)

    cqn = _rms(proj(_C_CQ, _C_CKV), gq_ref[...]).astype(BF16)
    qall = _mm(cqn, wuq_ref[...])
    n_nope = qall.shape[1] // 2
    q_nope = qall[:, :n_nope].astype(BF16)
    n_pe = (qall.shape[1] - n_nope) // 2
    q_pe = qall[:, n_nope:n_nope + n_pe] * cos + qall[:, n_nope + n_pe:] * sin
    rope_dim = n_pe // n_heads
    per_blk = LANES // rope_dim
    for j in range(n_heads // 2):
        ql = _mm(q_nope[:, j * LANES:(j + 1) * LANES], wuk_ref[j]) * mla_scale
        for hh in range(2):
            h = 2 * j + hh
            qcat_ref[h, :, :LANES] = ql[:, hh * LANES:(hh + 1) * LANES].astype(qcat_ref.dtype)
            blk = q_pe[:, (h // per_blk) * LANES:(h // per_blk + 1) * LANES]
            lo_l = (h % per_blk) * rope_dim
            keep = (lane >= lo_l) & (lane < lo_l + rope_dim)
            qcat_ref[h, :, LANES:] = (jnp.where(keep, blk, 0.0) * mla_scale).astype(qcat_ref.dtype)

    ckv = _rms(proj(_C_CKV, _C_MISC), gkv_ref[...])
    ckv32_ref[...] = ckv
    kcat_ref[:, :LANES] = ckv.astype(BF16)

    sga_ref[...] = _sigmoid(proj(_C_GA, _C_GB)).astype(sga_ref.dtype)
    sgb_ref[...] = _sigmoid(proj(_C_GB, _C_END)).astype(sgb_ref.dtype)


def _proj(x2d, pw, cos_t, sin_t, *, tm, seq, n_heads, hd, kv_rank, rope, q_dtype):
    m, d = x2d.shape
    tiles_per_seq = max(seq // tm, 1)
    seg = min(seq, tm)
    r = np.arange(tm)
    tri = jnp.asarray((r[None, :] <= r[:, None]) & (r[None, :] // seg == r[:, None] // seg), BF16)
    n_tab = cos_t.shape[0] // tm
    fw = n_heads * hd
    row = lambda w: pl.BlockSpec((tm, w), lambda i: (i, 0))
    head = lambda w: pl.BlockSpec((n_heads, tm, w), lambda i: (0, i, 0))
    out_shape = (
        jax.ShapeDtypeStruct((m, fw), F32), jax.ShapeDtypeStruct((m, fw), F32),
        jax.ShapeDtypeStruct((m, n_heads), F32), jax.ShapeDtypeStruct((m, n_heads), F32),
        jax.ShapeDtypeStruct((m, kv_rank), F32), jax.ShapeDtypeStruct((m, rope), F32),
        jax.ShapeDtypeStruct((n_heads, m, LANES), q_dtype), jax.ShapeDtypeStruct((n_heads, m, LANES), BF16),
        jax.ShapeDtypeStruct((m, fw), BF16),
        jax.ShapeDtypeStruct((n_heads, m, 2 * LANES), q_dtype), jax.ShapeDtypeStruct((m, 2 * LANES), BF16),
        jax.ShapeDtypeStruct((m, d), BF16), jax.ShapeDtypeStruct((m, d), BF16),
    )
    out_specs = (row(fw), row(fw), row(n_heads), row(n_heads), row(kv_rank), row(rope),
                 head(LANES), head(LANES), row(fw), head(2 * LANES), row(2 * LANES), row(d), row(d))
    tab = pl.BlockSpec((tm, cos_t.shape[1]), lambda i: (i % n_tab, 0))
    in_specs = [row(d), _const_spec(pw["g_attn"].shape), _const_spec(pw["w_all"].shape),
                _const_spec(pw["b_pad"].shape), _const_spec(pw["g_q"].shape), _const_spec(pw["w_uq_all"].shape),
                _const_spec(pw["g_kv"].shape), _const_spec(pw["w_uk_bd"].shape), _const_spec(tri.shape), tab, tab]
    body = functools.partial(_proj_kernel, tiles_per_seq=tiles_per_seq, fox_scale=hd ** -0.5,
                             mla_scale=pw["mla_scale"])
    return pl.pallas_call(
        body, out_shape=out_shape, grid=(m // tm,), in_specs=in_specs, out_specs=out_specs,
        scratch_shapes=[pltpu.VMEM((SUBLANES, LANES), F32)],
        compiler_params=pltpu.CompilerParams(dimension_semantics=("arbitrary",), vmem_limit_bytes=VMEM_LIMIT),
        name="proj",
    )(x2d, pw["g_attn"], pw["w_all"], pw["b_pad"], pw["g_q"], pw["w_uq_all"], pw["g_kv"], pw["w_uk_bd"],
      tri, cos_t, sin_t)


def _softmax_step(s, m, l, acc, v):
    m_new = jnp.maximum(m, jnp.max(s, axis=1, keepdims=True))
    corr = jnp.exp(m - m_new)
    p = jnp.exp(s - m_new)
    l_new = l * corr + jnp.sum(p, axis=1, keepdims=True)
    acc_new = acc * corr + _mm(p.astype(BF16), v)
    return m_new, l_new, acc_new


def _fox_kernel(qf_ref, kf_ref, v_ref, o_ref, *, tq):
    seq = v_ref.shape[0]
    half = LANES // 2
    row = lax.broadcasted_iota(jnp.int32, (tq, tq), 0)
    col = lax.broadcasted_iota(jnp.int32, (tq, tq), 1)
    for hh in range(2):
        def q_block(qi, _, hh=hh):
            q0 = pl.multiple_of(qi * tq, tq)
            q = qf_ref[hh, pl.ds(q0, tq), :]

            def k_block(ki, carry, masked):
                k0 = pl.multiple_of(ki * tq, tq)
                s = _nt(q, kf_ref[hh, pl.ds(k0, tq), :])
                if masked:
                    s = jnp.where(col <= row, s, -jnp.inf)
                return _softmax_step(s, *carry, v_ref[pl.ds(k0, tq), :])

            init = (jnp.full((tq, 1), -jnp.inf, F32), jnp.zeros((tq, 1), F32), jnp.zeros((tq, LANES), F32))
            carry = lax.fori_loop(0, qi, lambda ki, c: k_block(ki, c, False), init)
            _, l, acc = k_block(qi, carry, True)
            out = (acc / l).astype(o_ref.dtype)
            o_ref[pl.ds(q0, tq), hh * half:(hh + 1) * half] = out[:, hh * half:(hh + 1) * half]
            return 0
        lax.fori_loop(0, seq // tq, q_block, 0)


def _fox_attention(qf, kf, vb, *, batch, seq, tq):
    n_heads, m, _ = qf.shape
    fw = vb.shape[1]
    return pl.pallas_call(
        functools.partial(_fox_kernel, tq=tq),
        out_shape=jax.ShapeDtypeStruct((m, fw), BF16),
        grid=(batch, n_heads // 2),
        in_specs=[pl.BlockSpec((2, seq, LANES), lambda b, j: (j, b, 0)),
                  pl.BlockSpec((2, seq, LANES), lambda b, j: (j, b, 0)),
                  pl.BlockSpec((seq, LANES), lambda b, j: (b, j))],
        out_specs=pl.BlockSpec((seq, LANES), lambda b, j: (b, j)),
        compiler_params=pltpu.CompilerParams(dimension_semantics=("arbitrary", "arbitrary"),
                                             vmem_limit_bytes=VMEM_LIMIT),
        name="fox_attn",
    )(qf, kf, vb)


def _mla_kernel(q_ref, k_ref, o_ref, m_ref, l_ref, acc_ref, *, tq):
    qi = pl.program_id(1)
    n_heads = q_ref.shape[0]
    rows = n_heads * tq
    q = q_ref[...].reshape(rows, q_ref.shape[2])
    m_ref[...] = jnp.full(m_ref.shape, -jnp.inf, F32)
    l_ref[...] = jnp.zeros(l_ref.shape, F32)
    acc_ref[...] = jnp.zeros(acc_ref.shape, F32)

    def k_block(ki, masked):
        k0 = pl.multiple_of(ki * tq, tq)
        k = k_ref[pl.ds(k0, tq), :]
        s = _nt(q, k)
        if masked:
            tok = lax.broadcasted_iota(jnp.int32, (n_heads, tq, tq), 1).reshape(rows, tq)
            col = lax.broadcasted_iota(jnp.int32, (rows, tq), 1)
            s = jnp.where(col <= tok, s, -jnp.inf)
        m, l, acc = _softmax_step(s, m_ref[...], l_ref[...], acc_ref[...], k[:, :LANES])
        m_ref[...] = m
        l_ref[...] = l
        acc_ref[...] = acc

    def body(ki, _):
        k_block(ki, False)
        return 0
    lax.fori_loop(0, qi, body, 0)
    k_block(qi, True)
    o_ref[...] = (acc_ref[...] / l_ref[...]).reshape(o_ref.shape).astype(o_ref.dtype)


def _mla_attention(qcat, kcat, *, batch, seq, tq):
    n_heads, m, w = qcat.shape
    nq = seq // tq
    rows = n_heads * tq
    return pl.pallas_call(
        functools.partial(_mla_kernel, tq=tq),
        out_shape=jax.ShapeDtypeStruct((n_heads, m, LANES), BF16),
        grid=(batch, nq),
        in_specs=[pl.BlockSpec((n_heads, tq, w), lambda b, qi: (0, b * nq + qi, 0)),
                  pl.BlockSpec((seq, w), lambda b, qi: (b, 0))],
        out_specs=pl.BlockSpec((n_heads, tq, LANES), lambda b, qi: (0, b * nq + qi, 0)),
        scratch_shapes=[pltpu.VMEM((rows, 1), F32), pltpu.VMEM((rows, 1), F32), pltpu.VMEM((rows, LANES), F32)],
        compiler_params=pltpu.CompilerParams(dimension_semantics=("arbitrary", "arbitrary"),
                                             vmem_limit_bytes=VMEM_LIMIT),
        name="mla_attn",
    )(qcat, kcat)


def _sample_attn_kernel(pt_ref, qf_ref, kn_ref, vn_ref, cum_ref, qcat_ref, ckvn_ref, kpen_ref,
                        kc_ref, vc_ref, lf_ref, cc_ref, pc_ref, oa_ref, olat_ref,
                        kbuf, vbuf, lfbuf, cbuf, pbuf, sem, m_a, l_a, acc_a, m_b, l_b, acc_b, suf_ref,
                        *, pages_per_chunk, page, hd, rope):
    b = pl.program_id(0)
    nb = pl.num_programs(0)
    n_pages = pt_ref.shape[1]
    gpc = pages_per_chunk
    n_chunks = n_pages // gpc
    n_heads, t_new, _ = qf_ref.shape
    n = gpc * page
    rows_b = n_heads * t_new
    half = LANES // 2
    per_blk = LANES // rope

    def copies(bb, c, slot):
        out = []
        for jj in range(gpc):
            pg = pt_ref[bb, n_pages - (c + 1) * gpc + jj]
            kv_rows = pl.ds(pg * (page * n_heads), page * n_heads)
            lat_rows = pl.ds(pg * page, page)
            out.append(pltpu.make_async_copy(kc_ref.at[kv_rows, :],
                                             kbuf.at[slot, pl.ds(jj * page * n_heads, page * n_heads), :], sem.at[slot, 0]))
            out.append(pltpu.make_async_copy(vc_ref.at[kv_rows, :],
                                             vbuf.at[slot, pl.ds(jj * page * n_heads, page * n_heads), :], sem.at[slot, 1]))
            out.append(pltpu.make_async_copy(lf_ref.at[pg], lfbuf.at[slot, jj], sem.at[slot, 2]))
            out.append(pltpu.make_async_copy(cc_ref.at[lat_rows, :], cbuf.at[slot, pl.ds(jj * page, page), :], sem.at[slot, 3]))
            out.append(pltpu.make_async_copy(pc_ref.at[lat_rows, :], pbuf.at[slot, pl.ds(jj * page, page), :], sem.at[slot, 4]))
        return out

    def start_all(bb, c, slot):
        for cp in copies(bb, c, slot):
            cp.start()

    @pl.when(b == 0)
    def _():
        start_all(0, 0, 0)

    q_a = []
    for h in range(n_heads):
        o = 0 if h % 2 == 0 else half
        q_a.append(qf_ref[h][:, o:o + hd].astype(BF16))
    q_lat = qcat_ref[:, :, :LANES].reshape(rows_b, LANES).astype(BF16)
    q_pe = jnp.concatenate(
        [qcat_ref[h][:, LANES + (h % per_blk) * rope:LANES + (h % per_blk + 1) * rope] for h in range(n_heads)],
        axis=0).astype(BF16)
    cn = cum_ref[...]
    cn_col = [jnp.broadcast_to(cn[:, h:h + 1], (t_new, LANES)) for h in range(n_heads)]

    lane_t = lax.broadcasted_iota(jnp.int32, (t_new, LANES), 1)
    row_t = lax.broadcasted_iota(jnp.int32, (t_new, LANES), 0)
    pad_rows = LANES - t_new
    kn = kn_ref[...]
    vn = vn_ref[...]
    for h in range(n_heads):
        k_h = jnp.concatenate([kn[:, h * hd:(h + 1) * hd], jnp.zeros((pad_rows, hd), F32)], axis=0).astype(BF16)
        v_h = jnp.concatenate([vn[:, h * hd:(h + 1) * hd], jnp.zeros((pad_rows, hd), F32)], axis=0).astype(BF16)
        cn_row = jnp.sum(jnp.where(lane_t == row_t, cn_col[h], 0.0), axis=0, keepdims=True)
        s = _nt(q_a[h], k_h) + cn_col[h] - cn_row
        s = jnp.where(lane_t <= row_t, s, -jnp.inf)
        m = jnp.max(s, axis=1, keepdims=True)
        p = jnp.exp(s - m)
        m_a[h] = m
        l_a[h] = jnp.sum(p, axis=1, keepdims=True)
        acc_a[h] = _mm(p.astype(BF16), v_h)
    ckv_n = jnp.concatenate([ckvn_ref[...], jnp.zeros((pad_rows, LANES), F32)], axis=0).astype(BF16)
    kpe_n = jnp.concatenate([kpen_ref[...], jnp.zeros((pad_rows, rope), F32)], axis=0).astype(BF16)
    lane_b = lax.broadcasted_iota(jnp.int32, (rows_b, LANES), 1)
    tok_b = lax.broadcasted_iota(jnp.int32, (n_heads, t_new, LANES), 1).reshape(rows_b, LANES)
    s = jnp.where(lane_b <= tok_b, _nt(q_lat, ckv_n) + _nt(q_pe, kpe_n), -jnp.inf)
    m = jnp.max(s, axis=1, keepdims=True)
    p = jnp.exp(s - m)
    m_b[...] = m
    l_b[...] = jnp.sum(p, axis=1, keepdims=True)
    acc_b[...] = _mm(p.astype(BF16), ckv_n)
    suf_ref[...] = jnp.zeros_like(suf_ref)

    lane_p = lax.broadcasted_iota(jnp.int32, (gpc, n_heads, LANES), 2)

    def chunk(c, _):
        slot = (b * n_chunks + c) % 2
        for cp in copies(b, c, slot):
            cp.wait()

        @pl.when(c + 1 < n_chunks)
        def _():
            start_all(b, c + 1, 1 - slot)

        @pl.when((c + 1 == n_chunks) & (b + 1 < nb))
        def _():
            start_all(b + 1, 0, 1 - slot)

        lf = lfbuf[slot]
        incl = lf
        sh = 1
        while sh < page:
            incl = incl + jnp.where(lane_p < page - sh, pltpu.roll(incl, page - sh, 2), 0.0)
            sh *= 2
        excl = incl - lf
        run = suf_ref[...]
        bias = [None] * gpc
        for jj in reversed(range(gpc)):
            bias[jj] = excl[jj] + run
            run = run + jnp.broadcast_to(incl[jj][:, 0:1], run.shape)
        suf_ref[...] = run

        for h in range(n_heads):
            k_h = kbuf[slot, pl.ds(h, n, stride=n_heads), :].astype(BF16)
            v_h = vbuf[slot, pl.ds(h, n, stride=n_heads), :].astype(BF16)
            r_h = jnp.concatenate([jnp.broadcast_to(bias[jj][h:h + 1, :], (t_new, page)) for jj in range(gpc)], axis=1)
            cn_h = jnp.concatenate([cn_col[h]] * (n // LANES), axis=1) if n > LANES else cn_col[h]
            s = _nt(q_a[h], k_h) + cn_h + r_h
            m, l, acc = _softmax_step(s, m_a[h], l_a[h], acc_a[h], v_h)
            m_a[h] = m
            l_a[h] = l
            acc_a[h] = acc

        ck = cbuf[slot].astype(BF16)
        kp = pbuf[slot].astype(BF16)
        s = _nt(q_lat, ck) + _nt(q_pe, kp)
        m, l, acc = _softmax_step(s, m_b[...], l_b[...], acc_b[...], ck)
        m_b[...] = m
        l_b[...] = l
        acc_b[...] = acc
        return 0

    lax.fori_loop(0, n_chunks, chunk, 0)

    for h in range(n_heads):
        oa_ref[:, h * hd:(h + 1) * hd] = (acc_a[h] / l_a[h]).astype(oa_ref.dtype)
    olat_ref[...] = (acc_b[...] / l_b[...]).reshape(olat_ref.shape).astype(olat_ref.dtype)


def _sample_attention(page_table, qf, k32, v32, cum, qcat, ckv32, kpe32, kc, vc, lft, cc, pc, *,
                      t_new, page, pages_per_chunk):
    n_heads, ms, _ = qf.shape
    fw = k32.shape[1]
    hd = fw // n_heads
    rope = kpe32.shape[1]
    db = ms // t_new
    gpc = pages_per_chunk
    any_spec = pl.BlockSpec(memory_space=pl.ANY)
    row = lambda w: pl.BlockSpec((t_new, w), lambda b, pt: (b, 0))
    head = lambda w: pl.BlockSpec((n_heads, t_new, w), lambda b, pt: (0, b, 0))
    grid_spec = pltpu.PrefetchScalarGridSpec(
        num_scalar_prefetch=1, grid=(db,),
        in_specs=[head(LANES), row(fw), row(fw), row(n_heads), head(2 * LANES), row(LANES), row(rope),
                  any_spec, any_spec, any_spec, any_spec, any_spec],
        out_specs=(row(fw), head(LANES)),
        scratch_shapes=[
            pltpu.VMEM((2, gpc * page * n_heads, hd), F32), pltpu.VMEM((2, gpc * page * n_heads, hd), F32),
            pltpu.VMEM((2, gpc, n_heads, page), F32), pltpu.VMEM((2, gpc * page, LANES), F32),
            pltpu.VMEM((2, gpc * page, rope), F32), pltpu.SemaphoreType.DMA((2, 5)),
            pltpu.VMEM((n_heads, t_new, 1), F32), pltpu.VMEM((n_heads, t_new, 1), F32),
            pltpu.VMEM((n_heads, t_new, hd), F32),
            pltpu.VMEM((n_heads * t_new, 1), F32), pltpu.VMEM((n_heads * t_new, 1), F32),
            pltpu.VMEM((n_heads * t_new, LANES), F32), pltpu.VMEM((n_heads, page), F32),
        ])
    body = functools.partial(_sample_attn_kernel, pages_per_chunk=gpc, page=page, hd=hd, rope=rope)
    return pl.pallas_call(
        body, grid_spec=grid_spec,
        out_shape=(jax.ShapeDtypeStruct((ms, fw), F32), jax.ShapeDtypeStruct((n_heads, ms, LANES), F32)),
        compiler_params=pltpu.CompilerParams(dimension_semantics=("arbitrary",), vmem_limit_bytes=VMEM_LIMIT),
        name="sample_attn",
    )(page_table, qf, k32, v32, cum, qcat, ckv32, kpe32, kc, vc, lft, cc, pc)


def _route(logits_t, bias_col):
    n_exp, tm = logits_t.shape
    epg = n_exp // N_GROUPS
    score = _sigmoid(logits_t)
    biased = score + bias_col
    b3 = biased.reshape(N_GROUPS, epg, tm)
    sub = lax.broadcasted_iota(jnp.int32, b3.shape, 1)
    m1 = jnp.max(b3, axis=1, keepdims=True)
    i1 = jnp.min(jnp.where(b3 == m1, sub, epg), axis=1, keepdims=True)
    m2 = jnp.max(jnp.where(sub == i1, -jnp.inf, b3), axis=1, keepdims=True)
    g = (m1 + m2).reshape(N_GROUPS, tm)
    gidx = lax.broadcasted_iota(jnp.int32, g.shape, 0)
    gmask = jnp.zeros(g.shape, F32)
    for _ in range(TOPK_GROUPS):
        mg = jnp.max(g, axis=0, keepdims=True)
        ig = jnp.min(jnp.where(g == mg, gidx, N_GROUPS), axis=0, keepdims=True)
        sel = gidx == ig
        gmask = jnp.where(sel, 1.0, gmask)
        g = jnp.where(sel, -jnp.inf, g)
    emask = jnp.concatenate([jnp.broadcast_to(gmask[k:k + 1, :], (epg, tm)) for k in range(N_GROUPS)], axis=0)
    v = jnp.where(emask > 0, biased, -jnp.inf)
    eidx = lax.broadcasted_iota(jnp.int32, v.shape, 0)
    chosen = jnp.zeros(v.shape, F32)
    for _ in range(TOP_K):
        mv = jnp.max(v, axis=0, keepdims=True)
        iv = jnp.min(jnp.where(v == mv, eidx, n_exp), axis=0, keepdims=True)
        sel = eidx == iv
        chosen = jnp.where(sel, 1.0, chosen)
        v = jnp.where(sel, -jnp.inf, v)
    w = chosen * score
    return w / jnp.sum(w, axis=0, keepdims=True) * ROUTED_SCALE


def _post_kernel(x_ref, oa_ref, olat_ref, sga_ref, sgb_ref, wuv_ref, wba_ref, wbb_ref, wout_ref, gffn_ref,
                 wrh_ref, wrl_ref, rbias_ref, wsg_ref, wsu_ref, wsd_ref, x3_ref, hn_ref, gates_ref):
    n_heads = olat_ref.shape[0]
    tm = x_ref.shape[0]
    branch_a = _mm(oa_ref[...].astype(BF16), wba_ref[...])
    ob = jnp.concatenate(
        [_mm(jnp.concatenate([olat_ref[2 * j].astype(BF16), olat_ref[2 * j + 1].astype(BF16)], axis=1), wuv_ref[j])
         for j in range(n_heads // 2)], axis=1).astype(BF16)
    branch_b = _mm(ob, wbb_ref[...])
    merged = sga_ref[...].astype(F32) * branch_a + sgb_ref[...].astype(F32) * branch_b
    x2 = x_ref[...] + _mm(merged.astype(BF16), wout_ref[...])
    hn = _rms(x2, gffn_ref[...])
    hb = hn.astype(BF16)
    hn_ref[...] = hb
    h_lo = (hn - hb.astype(F32)).astype(BF16)
    wrh = wrh_ref[...]
    logits_t = _nt(wrh, hb) + _nt(wrh, h_lo) + _nt(wrl_ref[...], hb)
    gates_t = _route(logits_t, rbias_ref[...])
    n_exp = gates_t.shape[0]
    gates_ref[...] = jnp.concatenate([gates_t, jnp.zeros((LANES - n_exp, tm), F32)], axis=0).T
    sg = _mm(hb, wsg_ref[...])
    su = _mm(hb, wsu_ref[...])
    act = (sg * _sigmoid(sg) * su).astype(BF16)
    x3_ref[...] = x2 + _mm(act, wsd_ref[...])


def _post(x2d, oa, olat, sga, sgb, qw, *, tm):
    m, d = x2d.shape
    n_heads = olat.shape[0]
    row = lambda w: pl.BlockSpec((tm, w), lambda i: (i, 0))
    names = ("w_uv_bd", "w_ba", "w_bb", "w_out", "g_ffn", "w_r_hi", "w_r_lo", "r_bias", "w_sg", "w_su", "w_sd")
    in_specs = [row(d), row(oa.shape[1]), pl.BlockSpec((n_heads, tm, LANES), lambda i: (0, i, 0)), row(d), row(d)]
    in_specs += [_const_spec(qw[k].shape) for k in names]
    return pl.pallas_call(
        _post_kernel,
        out_shape=(jax.ShapeDtypeStruct((m, d), F32), jax.ShapeDtypeStruct((m, d), BF16),
                   jax.ShapeDtypeStruct((m, LANES), F32)),
        grid=(m // tm,), in_specs=in_specs, out_specs=(row(d), row(d), row(LANES)),
        compiler_params=pltpu.CompilerParams(dimension_semantics=("arbitrary",), vmem_limit_bytes=VMEM_LIMIT),
        name="post",
    )(x2d, oa, olat, sga, sgb, *[qw[k] for k in names])


def _moe_kernel(hn_ref, gates_ref, x3_ref, wg_ref, wu_ref, wd_ref, gfin_ref, y_ref):
    e = pl.program_id(1)

    @pl.when(e == 0)
    def _():
        y_ref[...] = x3_ref[...]

    hb = hn_ref[...]
    hg = _mm(hb, wg_ref[0].astype(BF16))
    hu = _mm(hb, wu_ref[0].astype(BF16))
    gates = gates_ref[...]
    lane = lax.broadcasted_iota(jnp.int32, gates.shape, 1)
    gate = jnp.sum(jnp.where(lane == e, gates, 0.0), axis=1, keepdims=True)
    act = (hg * _sigmoid(hg) * hu * gate).astype(BF16)
    y_ref[...] += _mm(act, wd_ref[0].astype(BF16))

    @pl.when(e == pl.num_programs(1) - 1)
    def _():
        y_ref[...] = _rms(y_ref[...], gfin_ref[...])


def _moe(hn, gates, x3, w_gate, w_up, w_down, g_final, *, tm):
    m, d = x3.shape
    n_exp, _, ff = w_gate.shape
    once = lambda w: pl.BlockSpec((tm, w), lambda i, e: (i, 0), pipeline_mode=pl.Buffered(1))
    return pl.pallas_call(
        _moe_kernel,
        out_shape=jax.ShapeDtypeStruct((m, d), F32),
        grid=(m // tm, n_exp),
        in_specs=[once(d), once(LANES), once(d),
                  pl.BlockSpec((1, d, ff), lambda i, e: (e, 0, 0)), pl.BlockSpec((1, d, ff), lambda i, e: (e, 0, 0)),
                  pl.BlockSpec((1, ff, d), lambda i, e: (e, 0, 0)), _const_spec(g_final.shape)],
        out_specs=pl.BlockSpec((tm, d), lambda i, e: (i, 0)),
        compiler_params=pltpu.CompilerParams(dimension_semantics=("arbitrary", "arbitrary"),
                                             vmem_limit_bytes=VMEM_LIMIT),
        name="moe",
    )(hn, gates, x3, w_gate, w_up, w_down, g_final)


def _prep_proj_weights(g_attn, w_in, b_forget, g_q_lat, w_uq, g_kv_lat, w_uk):
    q_rank, n_heads, qk = w_uq.shape
    kv_rank, _, nope = w_uk.shape
    rope = qk - nope
    half = rope // 2
    fw = (w_in.shape[1] - n_heads - q_rank - kv_rank - rope - 2 * w_in.shape[0]) // 3
    sizes = (fw, fw, fw, n_heads, q_rank, kv_rank, rope, w_in.shape[0], w_in.shape[0])
    qa, ka, va, fa, cq, ckv, kr, ga, gb = jnp.split(w_in, [int(v) for v in np.cumsum(sizes)[:-1]], axis=1)
    kr_sw = jnp.concatenate([-kr[:, half:], kr[:, :half]], axis=1)
    rep = LANES // rope
    misc = jnp.concatenate([fa, jnp.zeros((w_in.shape[0], LANES - n_heads), F32),
                            jnp.tile(kr, (1, rep)), jnp.tile(kr_sw, (1, rep))], axis=1)
    w_all = jnp.concatenate([qa, ka, va, cq, ckv, misc, ga, gb], axis=1).astype(BF16)
    assert w_all.shape[1] == _C_END, w_all.shape
    w_pe = w_uq[:, :, nope:]
    w_pe_sw = jnp.concatenate([-w_pe[..., half:], w_pe[..., :half]], axis=-1)
    w_uq_all = jnp.concatenate([w_uq[:, :, :nope].reshape(q_rank, -1), w_pe.reshape(q_rank, -1),
                                w_pe_sw.reshape(q_rank, -1)], axis=1).astype(BF16)
    wk = jnp.transpose(w_uk, (1, 2, 0))
    z = jnp.zeros_like(wk[0])
    w_uk_bd = jnp.stack([jnp.concatenate([jnp.concatenate([wk[2 * j], z], axis=1),
                                          jnp.concatenate([z, wk[2 * j + 1]], axis=1)], axis=0)
                         for j in range(n_heads // 2)]).astype(BF16)
    b_pad = jnp.concatenate([b_forget, jnp.zeros((LANES - n_heads,), F32)])[None, :]
    return dict(g_attn=g_attn[None, :], w_all=w_all, b_pad=b_pad, g_q=g_q_lat[None, :], w_uq_all=w_uq_all,
                g_kv=g_kv_lat[None, :], w_uk_bd=w_uk_bd, mla_scale=float(qk) ** -0.5)


def _prep_post_weights(w_uv, w_branch_a, w_branch_b, w_out, g_ffn, w_router, router_bias, w_sh_gate, w_sh_up,
                       w_sh_down):
    n_heads = w_uv.shape[1]
    wv = jnp.transpose(w_uv, (1, 0, 2))
    z = jnp.zeros_like(wv[0])
    w_uv_bd = jnp.stack([jnp.concatenate([jnp.concatenate([wv[2 * j], z], axis=1),
                                          jnp.concatenate([z, wv[2 * j + 1]], axis=1)], axis=0)
                         for j in range(n_heads // 2)]).astype(BF16)
    wr_t = w_router.T
    wr_hi = wr_t.astype(BF16)
    wr_lo = (wr_t - wr_hi.astype(F32)).astype(BF16)
    return dict(w_uv_bd=w_uv_bd, w_ba=w_branch_a.astype(BF16), w_bb=w_branch_b.astype(BF16),
                w_out=w_out.astype(BF16), g_ffn=g_ffn[None, :], w_r_hi=wr_hi, w_r_lo=wr_lo,
                r_bias=router_bias[:, None], w_sg=w_sh_gate.astype(BF16), w_su=w_sh_up.astype(BF16),
                w_sd=w_sh_down.astype(BF16))


def _rope_tables(pos, rope, width):
    half = rope // 2
    inv_freq = ROPE_BASE ** (-jnp.arange(half, dtype=F32) / half)
    ang = pos.astype(F32)[:, None] * inv_freq[None, :]
    cos = jnp.cos(ang)
    sin = jnp.sin(ang)
    reps = width // rope
    return jnp.tile(jnp.concatenate([cos, cos], axis=1), (1, reps)), jnp.tile(jnp.concatenate([sin, sin], axis=1), (1, reps))


PROJ_TM = 256
ATTN_TQ = 256
POST_TM = 256
MOE_TILES = 8
PAGES_PER_CHUNK = 8


def kernel(x_prompt, x_sample, cache_fox_k, cache_fox_v, cache_fox_logf, cache_mla_ckv, cache_mla_kpe, page_table, g_attn, w_in, b_forget, g_q_lat, w_uq, g_kv_lat, w_uk, w_uv, w_branch_a, w_branch_b, w_out, g_ffn, w_router, router_bias, w_exp_gate, w_exp_up, w_exp_down, w_sh_gate, w_sh_up, w_sh_down, g_final):
    batch, seq, d = x_prompt.shape
    db, t_new, _ = x_sample.shape
    depth = g_attn.shape[0]
    _, n_pool, page, n_heads, hd = cache_fox_k.shape
    kv_rank = cache_mla_ckv.shape[-1]
    rope = cache_mla_kpe.shape[-1]
    past_len = page_table.shape[1] * page
    mp, ms = batch * seq, db * t_new

    cos_p, sin_p = _rope_tables(jnp.arange(seq, dtype=jnp.int32), rope, 2 * LANES)
    pos_s = past_len + (jnp.arange(ms, dtype=jnp.int32) % t_new)
    cos_s, sin_s = _rope_tables(pos_s, rope, 2 * LANES)

    xp = x_prompt.reshape(mp, d)
    xs = x_sample.reshape(ms, d)
    st_p, st_s = [], []
    for l in range(depth):
        pw = _prep_proj_weights(g_attn[l], w_in[l], b_forget[l], g_q_lat[l], w_uq[l], g_kv_lat[l], w_uk[l])
        qw = _prep_post_weights(w_uv[l], w_branch_a[l], w_branch_b[l], w_out[l], g_ffn[l], w_router[l],
                                router_bias[l], w_sh_gate[l], w_sh_up[l], w_sh_down[l])
        dims = dict(n_heads=n_heads, hd=hd, kv_rank=kv_rank, rope=rope)

        (k_p, v_p, logf_p, _, ckv_p, kpe_p, qf_p, kf_p, vb_p, qcat_p, kcat_p, sga_p, sgb_p) = _proj(
            xp, pw, cos_p, sin_p, tm=PROJ_TM, seq=seq, q_dtype=BF16, **dims)
        oa_p = _fox_attention(qf_p, kf_p, vb_p, batch=batch, seq=seq, tq=ATTN_TQ)
        olat_p = _mla_attention(qcat_p, kcat_p, batch=batch, seq=seq, tq=ATTN_TQ)

        (k_s, v_s, logf_s, cum_s, ckv_s, kpe_s, qf_s, _, _, qcat_s, _, sga_s, sgb_s) = _proj(
            xs, pw, cos_s, sin_s, tm=PROJ_TM, seq=t_new, q_dtype=F32, **dims)
        kc = cache_fox_k[l].reshape(n_pool * page * n_heads, hd)
        vc = cache_fox_v[l].reshape(n_pool * page * n_heads, hd)
        lft = jnp.transpose(cache_fox_logf[l], (0, 2, 1))
        cc = cache_mla_ckv[l].reshape(n_pool * page, kv_rank)
        pc = cache_mla_kpe[l].reshape(n_pool * page, rope)
        oa_s, olat_s = _sample_attention(page_table, qf_s, k_s, v_s, cum_s, qcat_s, ckv_s, kpe_s, kc, vc, lft, cc, pc,
                                         t_new=t_new, page=page, pages_per_chunk=PAGES_PER_CHUNK)

        x3_p, hn_p, gates_p = _post(xp, oa_p, olat_p, sga_p, sgb_p, qw, tm=POST_TM)
        x3_s, hn_s, gates_s = _post(xs, oa_s, olat_s, sga_s, sgb_s, qw, tm=POST_TM)

        last = l == depth - 1
        g_out = g_final[None, :] if last else None
        assert last, "the final norm is fused into the last layer's MoE kernel; depth > 1 needs an un-normed variant"
        y = _moe(jnp.concatenate([hn_p, hn_s]), jnp.concatenate([gates_p, gates_s]), jnp.concatenate([x3_p, x3_s]),
                 w_exp_gate[l], w_exp_up[l], w_exp_down[l], g_out, tm=(mp + ms) // MOE_TILES)
        xp, xs = y[:mp], y[mp:]
        st_p.append((k_p.reshape(batch, seq, n_heads, hd), v_p.reshape(batch, seq, n_heads, hd),
                     logf_p.reshape(batch, seq, n_heads), ckv_p.reshape(batch, seq, kv_rank),
                     kpe_p.reshape(batch, seq, rope)))
        st_s.append((k_s.reshape(db, t_new, n_heads, hd), v_s.reshape(db, t_new, n_heads, hd),
                     logf_s.reshape(db, t_new, n_heads), ckv_s.reshape(db, t_new, kv_rank),
                     kpe_s.reshape(db, t_new, rope)))

    stack = lambda sts, k: jnp.stack([s[k] for s in sts])
    return (xp.reshape(batch, seq, d), xs.reshape(db, t_new, d),
            *[stack(st_p, k) for k in range(5)], *[stack(st_s, k) for k in range(5)])
```

```python
import functools

import jax
import jax.numpy as jnp
import numpy as np
from jax import lax
from jax.experimental import pallas as pl
from jax.experimental.pallas import tpu as pltpu

F32 = jnp.float32
BF16 = jnp.bfloat16

RMS_EPS = 1e-6
ROPE_BASE = 10000.0
ROUTED_SCALE = 2.5
N_GROUPS = 8
TOPK_GROUPS = 4
TOP_K = 8

LANES = 128
SUBLANES = 8
VMEM_LIMIT = 56 * 1024 * 1024

_C_Q, _C_K, _C_V, _C_CQ, _C_CKV, _C_MISC, _C_GA, _C_GB, _C_END = 0, 512, 1024, 1536, 1792, 1920, 2304, 3328, 4352
AUG = 3


def _nt(a, b):
    return lax.dot_general(a, b, (((1,), (1,)), ((), ())), preferred_element_type=F32)


def _mm(a, b):
    return jnp.dot(a, b, preferred_element_type=F32)


def _rms(x, g):
    return x * lax.rsqrt(jnp.mean(x * x, axis=-1, keepdims=True) + RMS_EPS) * g


def _sigmoid(z):
    return 1.0 / (1.0 + jnp.exp(-z))


def _split3(x):
    hi = x.astype(BF16).astype(F32)
    r1 = x - hi
    mid = r1.astype(BF16).astype(F32)
    lo = r1 - mid
    return hi, mid, lo


def _const_spec(shape):
    nd = len(shape)
    return pl.BlockSpec(shape, lambda *_: (0,) * nd, pipeline_mode=pl.Buffered(1))


def _proj_kernel(x_ref, g_ref, w_ref, b_ref, gq_ref, wuq_ref, gkv_ref, wuk_ref, tri_ref, cos_ref, sin_ref,
                 k32_ref, v32_ref, logf_ref, cum_ref, ckv32_ref, kpe32_ref, qf_ref, kf_ref, vb_ref,
                 qcat_ref, kcat_ref, sga_ref, sgb_ref, carry_ref, *, tiles_per_seq, fox_scale, mla_scale):
    i = pl.program_id(0)
    tm = x_ref.shape[0]
    n_heads = qf_ref.shape[0]
    hb = _rms(x_ref[...], g_ref[...]).astype(BF16)

    def proj(a, b):
        return _mm(hb, w_ref[:, a:b])

    cos = cos_ref[...]
    sin = sin_ref[...]
    lane = lax.broadcasted_iota(jnp.int32, (tm, LANES), 1)

    zm = proj(_C_MISC, _C_GA)
    z = zm[:, :LANES] + b_ref[...]
    logf = jnp.where(lane < n_heads, jnp.minimum(z, 0.0) - jnp.log1p(jnp.exp(-jnp.abs(z))), 0.0)
    logf_ref[...] = logf[:, :n_heads]
    tri = tri_ref[...]
    hi, mid, lo = _split3(logf)
    cum = _mm(tri, hi.astype(BF16)) + _mm(tri, mid.astype(BF16)) + _mm(tri, lo.astype(BF16))
    if tiles_per_seq > 1:
        @pl.when(i % tiles_per_seq == 0)
        def _():
            carry_ref[...] = jnp.zeros_like(carry_ref)
        cum = cum + carry_ref[0:1, :]
        carry_ref[...] = jnp.broadcast_to(cum[tm - 1:tm, :], carry_ref.shape)
    cum_ref[...] = cum[:, :n_heads]

    kpe = zm[:, LANES:2 * LANES] * cos[:, :LANES] + zm[:, 2 * LANES:] * sin[:, :LANES]
    kpe32_ref[...] = kpe[:, :kpe32_ref.shape[1]]
    kcat_ref[:, LANES:] = kpe.astype(BF16)

    zq = proj(_C_Q, _C_K)
    zk = proj(_C_K, _C_V)
    k32_ref[...] = zk
    for h in range(n_heads):
        even = h % 2 == 0
        o = LANES // 2 if even else 0
        data = (lane < LANES // 2) if even else (lane >= LANES // 2)
        c_hi, c_mid, c_lo = _split3(jnp.broadcast_to(cum[:, h:h + 1], (tm, LANES)))
        ones_q = (lane >= o + AUG) & (lane < o + 2 * AUG)
        ones_k = (lane >= o) & (lane < o + AUG)
        aug_q = jnp.where(lane == o, c_hi, jnp.where(lane == o + 1, c_mid, jnp.where(lane == o + 2, c_lo,
                          jnp.where(ones_q, 1.0, 0.0))))
        aug_k = jnp.where(lane == o + AUG, -c_hi, jnp.where(lane == o + AUG + 1, -c_mid,
                          jnp.where(lane == o + AUG + 2, -c_lo, jnp.where(ones_k, 1.0, 0.0))))
        pair = slice((h // 2) * LANES, (h // 2 + 1) * LANES)
        qf_ref[h] = jnp.where(data, zq[:, pair] * fox_scale, aug_q).astype(qf_ref.dtype)
        kf_ref[h] = jnp.where(data, zk[:, pair], aug_k).astype(kf_ref.dtype)

    zv = proj(_C_V, _C_CQ)
    v32_ref[...] = zv
    vb_ref[...] = zv.astype(vb_ref.dtype)

    cqn = _rms(proj(_C_CQ, _C_CKV), gq_ref[...]).astype(BF16)
    qall = _mm(cqn, wuq_ref[...])
    n_nope = qall.shape[1] // 2
    q_nope = qall[:, :n_nope].astype(BF16)
    n_pe = (qall.shape[1] - n_nope) // 2
    q_pe = qall[:, n_nope:n_nope + n_pe] * cos + qall[:, n_nope + n_pe:] * sin
    rope_dim = n_pe // n_heads
    per_blk = LANES // rope_dim
    for j in range(n_heads // 2):
        ql = _mm(q_nope[:, j * LANES:(j + 1) * LANES], wuk_ref[j]) * mla_scale
        for hh in range(2):
            h = 2 * j + hh
            qcat_ref[h, :, :LANES] = ql[:, hh * LANES:(hh + 1) * LANES].astype(qcat_ref.dtype)
            blk = q_pe[:, (h // per_blk) * LANES:(h // per_blk + 1) * LANES]
            lo_l = (h % per_blk) * rope_dim
            keep = (lane >= lo_l) & (lane < lo_l + rope_dim)
            qcat_ref[h, :, LANES:] = (jnp.where(keep, blk, 0.0) * mla_scale).astype(qcat_ref.dtype)

    ckv = _rms(proj(_C_CKV, _C_MISC), gkv_ref[...])
    ckv32_ref[...] = ckv
    kcat_ref[:, :LANES] = ckv.astype(BF16)

    sga_ref[...] = _sigmoid(proj(_C_GA, _C_GB)).astype(sga_ref.dtype)
    sgb_ref[...] = _sigmoid(proj(_C_GB, _C_END)).astype(sgb_ref.dtype)


def _proj(x2d, pw, cos_t, sin_t, *, tm, seq, n_heads, hd, kv_rank, rope, q_dtype):
    m, d = x2d.shape
    tiles_per_seq = max(seq // tm, 1)
    seg = min(seq, tm)
    r = np.arange(tm)
    tri = jnp.asarray((r[None, :] <= r[:, None]) & (r[None, :] // seg == r[:, None] // seg), BF16)
    n_tab = cos_t.shape[0] // tm
    fw = n_heads * hd
    row = lambda w: pl.BlockSpec((tm, w), lambda i: (i, 0))
    head = lambda w: pl.BlockSpec((n_heads, tm, w), lambda i: (0, i, 0))
    out_shape = (
        jax.ShapeDtypeStruct((m, fw), F32), jax.ShapeDtypeStruct((m, fw), F32),
        jax.ShapeDtypeStruct((m, n_heads), F32), jax.ShapeDtypeStruct((m, n_heads), F32),
        jax.ShapeDtypeStruct((m, kv_rank), F32), jax.ShapeDtypeStruct((m, rope), F32),
        jax.ShapeDtypeStruct((n_heads, m, LANES), q_dtype), jax.ShapeDtypeStruct((n_heads, m, LANES), BF16),
        jax.ShapeDtypeStruct((m, fw), BF16),
        jax.ShapeDtypeStruct((n_heads, m, 2 * LANES), q_dtype), jax.ShapeDtypeStruct((m, 2 * LANES), BF16),
        jax.ShapeDtypeStruct((m, d), BF16), jax.ShapeDtypeStruct((m, d), BF16),
    )
    out_specs = (row(fw), row(fw), row(n_heads), row(n_heads), row(kv_rank), row(rope),
                 head(LANES), head(LANES), row(fw), head(2 * LANES), row(2 * LANES), row(d), row(d))
    tab = pl.BlockSpec((tm, cos_t.shape[1]), lambda i: (i % n_tab, 0))
    in_specs = [row(d), _const_spec(pw["g_attn"].shape), _const_spec(pw["w_all"].shape),
                _const_spec(pw["b_pad"].shape), _const_spec(pw["g_q"].shape), _const_spec(pw["w_uq_all"].shape),
                _const_spec(pw["g_kv"].shape), _const_spec(pw["w_uk_bd"].shape), _const_spec(tri.shape), tab, tab]
    body = functools.partial(_proj_kernel, tiles_per_seq=tiles_per_seq, fox_scale=hd ** -0.5,
                             mla_scale=pw["mla_scale"])
    return pl.pallas_call(
        body, out_shape=out_shape, grid=(m // tm,), in_specs=in_specs, out_specs=out_specs,
        scratch_shapes=[pltpu.VMEM((SUBLANES, LANES), F32)],
        compiler_params=pltpu.CompilerParams(dimension_semantics=("arbitrary",), vmem_limit_bytes=VMEM_LIMIT),
        name="proj",
    )(x2d, pw["g_attn"], pw["w_all"], pw["b_pad"], pw["g_q"], pw["w_uq_all"], pw["g_kv"], pw["w_uk_bd"],
      tri, cos_t, sin_t)


def _online(s, m_ref, l_ref, idx):
    rows, cols = s.shape
    m_old = m_ref[idx]
    m_new = jnp.maximum(m_old, jnp.broadcast_to(jnp.max(s, axis=1, keepdims=True), (rows, LANES)))
    corr = jnp.exp(m_old - m_new)
    p = jnp.exp(s - jnp.concatenate([m_new] * (cols // LANES), axis=1))
    m_ref[idx] = m_new
    l_ref[idx] = l_ref[idx] * corr + jnp.broadcast_to(jnp.sum(p, axis=1, keepdims=True), (rows, LANES))
    return p.astype(BF16), corr


def _fox_kernel(qf_ref, kf_ref, v_ref, o_ref, m_ref, l_ref, acc_ref, *, tq):
    seq = v_ref.shape[0]
    row = lax.broadcasted_iota(jnp.int32, (tq, tq), 0)
    col = lax.broadcasted_iota(jnp.int32, (tq, tq), 1)
    lane = lax.broadcasted_iota(jnp.int32, (tq, LANES), 1)

    def q_block(qi, _):
        q0 = pl.multiple_of(qi * tq, tq)
        m_ref[...] = jnp.full(m_ref.shape, -jnp.inf, F32)
        l_ref[...] = jnp.zeros(l_ref.shape, F32)
        acc_ref[...] = jnp.zeros(acc_ref.shape, F32)

        def k_block(ki, masked):
            k0 = pl.multiple_of(ki * tq, tq)
            s = [_nt(qf_ref[hh, pl.ds(q0, tq), :], kf_ref[hh, pl.ds(k0, tq), :]) for hh in range(2)]
            if masked:
                s = [jnp.where(col <= row, x, -jnp.inf) for x in s]
            pc = [_online(s[hh], m_ref, l_ref, hh) for hh in range(2)]
            v = v_ref[pl.ds(k0, tq), :]
            for hh in range(2):
                acc_ref[hh] = acc_ref[hh] * pc[hh][1] + _mm(pc[hh][0], v)

        def body(ki, c):
            k_block(ki, False)
            return c
        lax.fori_loop(0, qi, body, 0)
        k_block(qi, True)
        out = jnp.where(lane < LANES // 2, acc_ref[0] / l_ref[0], acc_ref[1] / l_ref[1])
        o_ref[pl.ds(q0, tq), :] = out.astype(o_ref.dtype)
        return 0
    lax.fori_loop(0, seq // tq, q_block, 0)


def _fox_attention(qf, kf, vb, *, batch, seq, tq):
    n_heads, m, _ = qf.shape
    fw = vb.shape[1]
    return pl.pallas_call(
        functools.partial(_fox_kernel, tq=tq),
        out_shape=jax.ShapeDtypeStruct((m, fw), BF16),
        grid=(batch, n_heads // 2),
        in_specs=[pl.BlockSpec((2, seq, LANES), lambda b, j: (j, b, 0)),
                  pl.BlockSpec((2, seq, LANES), lambda b, j: (j, b, 0)),
                  pl.BlockSpec((seq, LANES), lambda b, j: (b, j))],
        out_specs=pl.BlockSpec((seq, LANES), lambda b, j: (b, j)),
        scratch_shapes=[pltpu.VMEM((2, tq, LANES), F32), pltpu.VMEM((2, tq, LANES), F32),
                        pltpu.VMEM((2, tq, LANES), F32)],
        compiler_params=pltpu.CompilerParams(dimension_semantics=("arbitrary", "arbitrary"),
                                             vmem_limit_bytes=VMEM_LIMIT),
        name="fox_attn",
    )(qf, kf, vb)


def _mla_kernel(q_ref, k_ref, o_ref, m_ref, l_ref, acc_ref, *, tq):
    qi = pl.program_id(1)
    n_heads = q_ref.shape[0]
    rows = n_heads * tq
    q = q_ref[...].reshape(rows, q_ref.shape[2])
    m_ref[...] = jnp.full(m_ref.shape, -jnp.inf, F32)
    l_ref[...] = jnp.zeros(l_ref.shape, F32)
    acc_ref[...] = jnp.zeros(acc_ref.shape, F32)

    def k_block(ki, masked):
        k0 = pl.multiple_of(ki * tq, tq)
        k = k_ref[pl.ds(k0, tq), :]
        s = _nt(q, k)
        if masked:
            tok = lax.broadcasted_iota(jnp.int32, (n_heads, tq, tq), 1).reshape(rows, tq)
            col = lax.broadcasted_iota(jnp.int32, (rows, tq), 1)
            s = jnp.where(col <= tok, s, -jnp.inf)
        p, corr = _online(s, m_ref, l_ref, ...)
        acc_ref[...] = acc_ref[...] * corr + _mm(p, k[:, :LANES])

    def body(ki, c):
        k_block(ki, False)
        return c
    lax.fori_loop(0, qi, body, 0)
    k_block(qi, True)
    o_ref[...] = (acc_ref[...] / l_ref[...]).reshape(o_ref.shape).astype(o_ref.dtype)


def _mla_attention(qcat, kcat, *, batch, seq, tq):
    n_heads, m, w = qcat.shape
    nq = seq // tq
    rows = n_heads * tq
    return pl.pallas_call(
        functools.partial(_mla_kernel, tq=tq),
        out_shape=jax.ShapeDtypeStruct((n_heads, m, LANES), BF16),
        grid=(batch, nq),
        in_specs=[pl.BlockSpec((n_heads, tq, w), lambda b, qi: (0, b * nq + qi, 0)),
                  pl.BlockSpec((seq, w), lambda b, qi: (b, 0))],
        out_specs=pl.BlockSpec((n_heads, tq, LANES), lambda b, qi: (0, b * nq + qi, 0)),
        scratch_shapes=[pltpu.VMEM((rows, LANES), F32), pltpu.VMEM((rows, LANES), F32), pltpu.VMEM((rows, LANES), F32)],
        compiler_params=pltpu.CompilerParams(dimension_semantics=("arbitrary", "arbitrary"),
                                             vmem_limit_bytes=VMEM_LIMIT),
        name="mla_attn",
    )(qcat, kcat)


def _sample_attn_kernel(pt_ref, qf_ref, kn_ref, vn_ref, cum_ref, qcat_ref, ckvn_ref, kpen_ref,
                        kc_ref, vc_ref, lf_ref, cc_ref, pc_ref, tri_ref, oa_ref, olat_ref,
                        kbuf, vbuf, lfbuf, cbuf, pbuf, sem, m_a, l_a, acc_a, m_b, l_b, acc_b, suf_ref,
                        *, pages_per_chunk, page, hd, rope):
    b = pl.program_id(0)
    nb = pl.num_programs(0)
    n_pages = pt_ref.shape[1]
    gpc = pages_per_chunk
    n_chunks = n_pages // gpc
    n_heads, t_new, _ = qf_ref.shape
    rows_b = n_heads * t_new
    half = LANES // 2
    per_blk = LANES // rope

    def copies(bb, c, slot):
        out = []
        for jj in range(gpc):
            pg = pt_ref[bb, n_pages - (c + 1) * gpc + jj]
            out.append(pltpu.make_async_copy(kc_ref.at[pg], kbuf.at[slot, jj], sem.at[slot, 0]))
            out.append(pltpu.make_async_copy(vc_ref.at[pg], vbuf.at[slot, jj], sem.at[slot, 1]))
            out.append(pltpu.make_async_copy(lf_ref.at[pg], lfbuf.at[slot, jj], sem.at[slot, 2]))
            out.append(pltpu.make_async_copy(cc_ref.at[pg], cbuf.at[slot, jj], sem.at[slot, 3]))
            out.append(pltpu.make_async_copy(pc_ref.at[pg], pbuf.at[slot, jj], sem.at[slot, 4]))
        return out

    def start_all(bb, c, slot):
        for cp in copies(bb, c, slot):
            cp.start()

    @pl.when(b == 0)
    def _():
        start_all(0, 0, 0)

    q_a = []
    for h in range(n_heads):
        o = 0 if h % 2 == 0 else half
        q_a.append(qf_ref[h][:, o:o + hd].astype(BF16))
    q_lat = qcat_ref[:, :, :LANES].reshape(rows_b, LANES).astype(BF16)
    q_pe = jnp.concatenate(
        [qcat_ref[h][:, LANES + (h % per_blk) * rope:LANES + (h % per_blk + 1) * rope] for h in range(n_heads)],
        axis=0).astype(BF16)
    cn = cum_ref[...]
    cn_col = [jnp.broadcast_to(cn[:, h:h + 1], (t_new, LANES)) for h in range(n_heads)]

    lane_t = lax.broadcasted_iota(jnp.int32, (t_new, LANES), 1)
    row_t = lax.broadcasted_iota(jnp.int32, (t_new, LANES), 0)
    pad_rows = LANES - t_new
    kn = kn_ref[...]
    vn = vn_ref[...]
    pad = lambda x: jnp.concatenate([x, jnp.zeros((pad_rows, x.shape[1]), F32)], axis=0).astype(BF16)
    s_new = [_nt(q_a[h], pad(kn[:, h * hd:(h + 1) * hd])) for h in range(n_heads)]
    ckv_n = pad(ckvn_ref[...])
    lane_b = lax.broadcasted_iota(jnp.int32, (rows_b, LANES), 1)
    tok_b = lax.broadcasted_iota(jnp.int32, (n_heads, t_new, LANES), 1).reshape(rows_b, LANES)
    s_b = jnp.where(lane_b <= tok_b, _nt(q_lat, ckv_n) + _nt(q_pe, pad(kpen_ref[...])), -jnp.inf)
    p_new = []
    for h in range(n_heads):
        cn_row = jnp.sum(jnp.where(lane_t == row_t, cn_col[h], 0.0), axis=0, keepdims=True)
        s = jnp.where(lane_t <= row_t, s_new[h] + cn_col[h] - cn_row, -jnp.inf)
        m = jnp.max(s, axis=1, keepdims=True)
        p = jnp.exp(s - m)
        m_a[h] = m
        l_a[h] = jnp.sum(p, axis=1, keepdims=True)
        p_new.append(p.astype(BF16))
    m = jnp.max(s_b, axis=1, keepdims=True)
    p = jnp.exp(s_b - m)
    m_b[...] = m
    l_b[...] = jnp.sum(p, axis=1, keepdims=True)
    for h in range(n_heads):
        acc_a[h] = _mm(p_new[h], pad(vn[:, h * hd:(h + 1) * hd]))
    acc_b[...] = _mm(p.astype(BF16), ckv_n)
    suf_ref[...] = jnp.zeros_like(suf_ref)

    def online(s, m_ref, l_ref, idx):
        m_old = m_ref[idx]
        m_new = jnp.maximum(m_old, jnp.max(s, axis=1, keepdims=True))
        corr = jnp.exp(m_old - m_new)
        p = jnp.exp(s - m_new)
        m_ref[idx] = m_new
        l_ref[idx] = l_ref[idx] * corr + jnp.sum(p, axis=1, keepdims=True)
        return p.astype(BF16), corr

    def chunk(c, _):
        slot = (b * n_chunks + c) % 2
        for cp in copies(b, c, slot):
            cp.wait()

        @pl.when(c + 1 < n_chunks)
        def _():
            start_all(b, c + 1, 1 - slot)

        @pl.when((c + 1 == n_chunks) & (b + 1 < nb))
        def _():
            start_all(b + 1, 0, 1 - slot)

        s_a = []
        for h in range(n_heads):
            k_t = jnp.concatenate([kbuf[slot, jj, h].astype(BF16) for jj in range(gpc)], axis=1)
            s_a.append(_mm(q_a[h], k_t))
        ck = jnp.concatenate([cbuf[slot, jj].astype(BF16) for jj in range(gpc)], axis=0)
        kp_t = jnp.concatenate([pbuf[slot, jj].astype(BF16) for jj in range(gpc)], axis=1)
        s_b = _nt(q_lat, ck) + _mm(q_pe, kp_t)

        lf = lfbuf[slot]
        rows = gpc * n_heads
        hi, mid, lo = _split3(lf.reshape(rows, page))
        cs = _mm(jnp.concatenate([hi, mid, lo], axis=0).astype(BF16), tri_ref[...])
        cs = cs[:rows] + cs[rows:2 * rows] + cs[2 * rows:]
        excl = cs[:, :page].reshape(gpc, n_heads, page) - lf
        tot = cs[:, page:].reshape(gpc, n_heads, page)
        run = suf_ref[...]
        bias = [None] * gpc
        for jj in reversed(range(gpc)):
            bias[jj] = excl[jj] + run
            run = run + tot[jj]
        suf_ref[...] = run

        p_a, corr_a = [], []
        for h in range(n_heads):
            r_h = jnp.concatenate([jnp.broadcast_to(bias[jj][h:h + 1, :], (t_new, page)) for jj in range(gpc)], axis=1)
            p, corr = online(s_a[h] + cn[:, h:h + 1] + r_h, m_a, l_a, h)
            p_a.append(p)
            corr_a.append(corr)
        p_b, corr_b = online(s_b, m_b, l_b, ...)

        for h in range(n_heads):
            v_t = jnp.concatenate([vbuf[slot, jj, h].astype(BF16) for jj in range(gpc)], axis=1)
            acc_a[h] = acc_a[h] * corr_a[h] + _nt(p_a[h], v_t)
        acc_b[...] = acc_b[...] * corr_b + _mm(p_b, ck)
        return 0

    lax.fori_loop(0, n_chunks, chunk, 0)

    for h in range(n_heads):
        oa_ref[:, h * hd:(h + 1) * hd] = (acc_a[h] / l_a[h]).astype(oa_ref.dtype)
    olat_ref[...] = (acc_b[...] / l_b[...]).reshape(olat_ref.shape).astype(olat_ref.dtype)


def _sample_attention(page_table, qf, k32, v32, cum, qcat, ckv32, kpe32, kc, vc, lft, cc, pc, *,
                      t_new, page, pages_per_chunk):
    n_heads, ms, _ = qf.shape
    fw = k32.shape[1]
    hd = fw // n_heads
    rope = kpe32.shape[1]
    db = ms // t_new
    gpc = pages_per_chunk
    any_spec = pl.BlockSpec(memory_space=pl.ANY)
    row = lambda w: pl.BlockSpec((t_new, w), lambda b, pt: (b, 0))
    head = lambda w: pl.BlockSpec((n_heads, t_new, w), lambda b, pt: (0, b, 0))
    grid_spec = pltpu.PrefetchScalarGridSpec(
        num_scalar_prefetch=1, grid=(db,),
        in_specs=[head(LANES), row(fw), row(fw), row(n_heads), head(2 * LANES), row(LANES), row(rope),
                  any_spec, any_spec, any_spec, any_spec, any_spec,
                  pl.BlockSpec((page, 2 * page), lambda b, pt: (0, 0), pipeline_mode=pl.Buffered(1))],
        out_specs=(row(fw), head(LANES)),
        scratch_shapes=[
            pltpu.VMEM((2, gpc, n_heads, hd, page), F32), pltpu.VMEM((2, gpc, n_heads, hd, page), F32),
            pltpu.VMEM((2, gpc, n_heads, page), F32), pltpu.VMEM((2, gpc, page, LANES), F32),
            pltpu.VMEM((2, gpc, rope, page), F32), pltpu.SemaphoreType.DMA((2, 5)),
            pltpu.VMEM((n_heads, t_new, 1), F32), pltpu.VMEM((n_heads, t_new, 1), F32),
            pltpu.VMEM((n_heads, t_new, hd), F32),
            pltpu.VMEM((n_heads * t_new, 1), F32), pltpu.VMEM((n_heads * t_new, 1), F32),
            pltpu.VMEM((n_heads * t_new, LANES), F32), pltpu.VMEM((n_heads, page), F32),
        ])
    r = np.arange(page)
    tri = jnp.asarray(np.concatenate([r[:, None] >= r[None, :], np.ones((page, page), bool)], axis=1), BF16)
    body = functools.partial(_sample_attn_kernel, pages_per_chunk=gpc, page=page, hd=hd, rope=rope)
    return pl.pallas_call(
        body, grid_spec=grid_spec,
        out_shape=(jax.ShapeDtypeStruct((ms, fw), F32), jax.ShapeDtypeStruct((n_heads, ms, LANES), F32)),
        compiler_params=pltpu.CompilerParams(dimension_semantics=("arbitrary",), vmem_limit_bytes=VMEM_LIMIT),
        name="sample_attn",
    )(page_table, qf, k32, v32, cum, qcat, ckv32, kpe32, kc, vc, lft, cc, pc, tri)


def _route(logits_t, bias_col):
    n_exp, tm = logits_t.shape
    epg = n_exp // N_GROUPS
    score = _sigmoid(logits_t)
    biased = score + bias_col
    b3 = biased.reshape(N_GROUPS, epg, tm)
    sub = lax.broadcasted_iota(jnp.int32, b3.shape, 1)
    m1 = jnp.max(b3, axis=1, keepdims=True)
    i1 = jnp.min(jnp.where(b3 == m1, sub, epg), axis=1, keepdims=True)
    m2 = jnp.max(jnp.where(sub == i1, -jnp.inf, b3), axis=1, keepdims=True)
    g = (m1 + m2).reshape(N_GROUPS, tm)
    gidx = lax.broadcasted_iota(jnp.int32, g.shape, 0)
    gmask = jnp.zeros(g.shape, F32)
    for _ in range(TOPK_GROUPS):
        mg = jnp.max(g, axis=0, keepdims=True)
        ig = jnp.min(jnp.where(g == mg, gidx, N_GROUPS), axis=0, keepdims=True)
        sel = gidx == ig
        gmask = jnp.where(sel, 1.0, gmask)
        g = jnp.where(sel, -jnp.inf, g)
    emask = jnp.concatenate([jnp.broadcast_to(gmask[k:k + 1, :], (epg, tm)) for k in range(N_GROUPS)], axis=0)
    v = jnp.where(emask > 0, biased, -jnp.inf)
    eidx = lax.broadcasted_iota(jnp.int32, v.shape, 0)
    chosen = jnp.zeros(v.shape, F32)
    for _ in range(TOP_K):
        mv = jnp.max(v, axis=0, keepdims=True)
        iv = jnp.min(jnp.where(v == mv, eidx, n_exp), axis=0, keepdims=True)
        sel = eidx == iv
        chosen = jnp.where(sel, 1.0, chosen)
        v = jnp.where(sel, -jnp.inf, v)
    w = chosen * score
    return w / jnp.sum(w, axis=0, keepdims=True) * ROUTED_SCALE


def _post_kernel(x_ref, oa_ref, olat_ref, sga_ref, sgb_ref, wuv_ref, wba_ref, wbb_ref, wout_ref, gffn_ref,
                 wrh_ref, wrl_ref, rbias_ref, wsg_ref, wsu_ref, wsd_ref, x3_ref, hn_ref, gates_ref):
    n_heads = olat_ref.shape[0]
    tm = x_ref.shape[0]
    branch_a = _mm(oa_ref[...].astype(BF16), wba_ref[...])
    ob = jnp.concatenate(
        [_mm(jnp.concatenate([olat_ref[2 * j].astype(BF16), olat_ref[2 * j + 1].astype(BF16)], axis=1), wuv_ref[j])
         for j in range(n_heads // 2)], axis=1).astype(BF16)
    branch_b = _mm(ob, wbb_ref[...])
    merged = sga_ref[...].astype(F32) * branch_a + sgb_ref[...].astype(F32) * branch_b
    x2 = x_ref[...] + _mm(merged.astype(BF16), wout_ref[...])
    hn = _rms(x2, gffn_ref[...])
    hb = hn.astype(BF16)
    hn_ref[...] = hb
    h_lo = (hn - hb.astype(F32)).astype(BF16)
    wrh = wrh_ref[...]
    logits_t = _nt(wrh, hb) + _nt(wrh, h_lo) + _nt(wrl_ref[...], hb)
    gates_t = _route(logits_t, rbias_ref[...])
    n_exp = gates_t.shape[0]
    gates_ref[...] = jnp.concatenate([gates_t, jnp.zeros((LANES - n_exp, tm), F32)], axis=0).T
    sg = _mm(hb, wsg_ref[...])
    su = _mm(hb, wsu_ref[...])
    act = (sg * _sigmoid(sg) * su).astype(BF16)
    x3_ref[...] = x2 + _mm(act, wsd_ref[...])


def _post(x2d, oa, olat, sga, sgb, qw, *, tm):
    m, d = x2d.shape
    n_heads = olat.shape[0]
    row = lambda w: pl.BlockSpec((tm, w), lambda i: (i, 0))
    names = ("w_uv_bd", "w_ba", "w_bb", "w_out", "g_ffn", "w_r_hi", "w_r_lo", "r_bias", "w_sg", "w_su", "w_sd")
    in_specs = [row(d), row(oa.shape[1]), pl.BlockSpec((n_heads, tm, LANES), lambda i: (0, i, 0)), row(d), row(d)]
    in_specs += [_const_spec(qw[k].shape) for k in names]
    return pl.pallas_call(
        _post_kernel,
        out_shape=(jax.ShapeDtypeStruct((m, d), F32), jax.ShapeDtypeStruct((m, d), BF16),
                   jax.ShapeDtypeStruct((m, LANES), F32)),
        grid=(m // tm,), in_specs=in_specs, out_specs=(row(d), row(d), row(LANES)),
        compiler_params=pltpu.CompilerParams(dimension_semantics=("arbitrary",), vmem_limit_bytes=VMEM_LIMIT),
        name="post",
    )(x2d, oa, olat, sga, sgb, *[qw[k] for k in names])


def _moe_kernel(hn_ref, gates_ref, x3_ref, wg_ref, wu_ref, wd_ref, gfin_ref, y_ref):
    e = pl.program_id(1)

    @pl.when(e == 0)
    def _():
        y_ref[...] = x3_ref[...]

    hb = hn_ref[...]
    hg = _mm(hb, wg_ref[0].astype(BF16))
    hu = _mm(hb, wu_ref[0].astype(BF16))
    gates = gates_ref[...]
    lane = lax.broadcasted_iota(jnp.int32, gates.shape, 1)
    gate = jnp.sum(jnp.where(lane == e, gates, 0.0), axis=1, keepdims=True)
    act = (hg * _sigmoid(hg) * hu * gate).astype(BF16)
    y_ref[...] += _mm(act, wd_ref[0].astype(BF16))

    @pl.when(e == pl.num_programs(1) - 1)
    def _():
        y_ref[...] = _rms(y_ref[...], gfin_ref[...])


def _moe(hn, gates, x3, w_gate, w_up, w_down, g_final, *, tm):
    m, d = x3.shape
    n_exp, _, ff = w_gate.shape
    once = lambda w: pl.BlockSpec((tm, w), lambda i, e: (i, 0), pipeline_mode=pl.Buffered(1))
    return pl.pallas_call(
        _moe_kernel,
        out_shape=jax.ShapeDtypeStruct((m, d), F32),
        grid=(m // tm, n_exp),
        in_specs=[once(d), once(LANES), once(d),
                  pl.BlockSpec((1, d, ff), lambda i, e: (e, 0, 0)), pl.BlockSpec((1, d, ff), lambda i, e: (e, 0, 0)),
                  pl.BlockSpec((1, ff, d), lambda i, e: (e, 0, 0)), _const_spec(g_final.shape)],
        out_specs=pl.BlockSpec((tm, d), lambda i, e: (i, 0)),
        compiler_params=pltpu.CompilerParams(dimension_semantics=("arbitrary", "arbitrary"),
                                             vmem_limit_bytes=VMEM_LIMIT),
        name="moe",
    )(hn, gates, x3, w_gate, w_up, w_down, g_final)


def _prep_proj_weights(g_attn, w_in, b_forget, g_q_lat, w_uq, g_kv_lat, w_uk):
    q_rank, n_heads, qk = w_uq.shape
    kv_rank, _, nope = w_uk.shape
    rope = qk - nope
    half = rope // 2
    fw = (w_in.shape[1] - n_heads - q_rank - kv_rank - rope - 2 * w_in.shape[0]) // 3
    sizes = (fw, fw, fw, n_heads, q_rank, kv_rank, rope, w_in.shape[0], w_in.shape[0])
    qa, ka, va, fa, cq, ckv, kr, ga, gb = jnp.split(w_in, [int(v) for v in np.cumsum(sizes)[:-1]], axis=1)
    kr_sw = jnp.concatenate([-kr[:, half:], kr[:, :half]], axis=1)
    rep = LANES // rope
    misc = jnp.concatenate([fa, jnp.zeros((w_in.shape[0], LANES - n_heads), F32),
                            jnp.tile(kr, (1, rep)), jnp.tile(kr_sw, (1, rep))], axis=1)
    w_all = jnp.concatenate([qa, ka, va, cq, ckv, misc, ga, gb], axis=1).astype(BF16)
    assert w_all.shape[1] == _C_END, w_all.shape
    w_pe = w_uq[:, :, nope:]
    w_pe_sw = jnp.concatenate([-w_pe[..., half:], w_pe[..., :half]], axis=-1)
    w_uq_all = jnp.concatenate([w_uq[:, :, :nope].reshape(q_rank, -1), w_pe.reshape(q_rank, -1),
                                w_pe_sw.reshape(q_rank, -1)], axis=1).astype(BF16)
    wk = jnp.transpose(w_uk, (1, 2, 0))
    z = jnp.zeros_like(wk[0])
    w_uk_bd = jnp.stack([jnp.concatenate([jnp.concatenate([wk[2 * j], z], axis=1),
                                          jnp.concatenate([z, wk[2 * j + 1]], axis=1)], axis=0)
                         for j in range(n_heads // 2)]).astype(BF16)
    b_pad = jnp.concatenate([b_forget, jnp.zeros((LANES - n_heads,), F32)])[None, :]
    return dict(g_attn=g_attn[None, :], w_all=w_all, b_pad=b_pad, g_q=g_q_lat[None, :], w_uq_all=w_uq_all,
                g_kv=g_kv_lat[None, :], w_uk_bd=w_uk_bd, mla_scale=float(qk) ** -0.5)


def _prep_post_weights(w_uv, w_branch_a, w_branch_b, w_out, g_ffn, w_router, router_bias, w_sh_gate, w_sh_up,
                       w_sh_down):
    n_heads = w_uv.shape[1]
    wv = jnp.transpose(w_uv, (1, 0, 2))
    z = jnp.zeros_like(wv[0])
    w_uv_bd = jnp.stack([jnp.concatenate([jnp.concatenate([wv[2 * j], z], axis=1),
                                          jnp.concatenate([z, wv[2 * j + 1]], axis=1)], axis=0)
                         for j in range(n_heads // 2)]).astype(BF16)
    wr_t = w_router.T
    wr_hi = wr_t.astype(BF16)
    wr_lo = (wr_t - wr_hi.astype(F32)).astype(BF16)
    return dict(w_uv_bd=w_uv_bd, w_ba=w_branch_a.astype(BF16), w_bb=w_branch_b.astype(BF16),
                w_out=w_out.astype(BF16), g_ffn=g_ffn[None, :], w_r_hi=wr_hi, w_r_lo=wr_lo,
                r_bias=router_bias[:, None], w_sg=w_sh_gate.astype(BF16), w_su=w_sh_up.astype(BF16),
                w_sd=w_sh_down.astype(BF16))


def _rope_tables(pos, rope, width):
    half = rope // 2
    inv_freq = ROPE_BASE ** (-jnp.arange(half, dtype=F32) / half)
    ang = pos.astype(F32)[:, None] * inv_freq[None, :]
    cos = jnp.cos(ang)
    sin = jnp.sin(ang)
    reps = width // rope
    return jnp.tile(jnp.concatenate([cos, cos], axis=1), (1, reps)), jnp.tile(jnp.concatenate([sin, sin], axis=1), (1, reps))


PROJ_TM = 256
FOX_TQ = 512
MLA_TQ = 256
POST_TM = 256
MOE_TILES = 8
PAGES_PER_CHUNK = 8


def kernel(x_prompt, x_sample, cache_fox_k, cache_fox_v, cache_fox_logf, cache_mla_ckv, cache_mla_kpe, page_table, g_attn, w_in, b_forget, g_q_lat, w_uq, g_kv_lat, w_uk, w_uv, w_branch_a, w_branch_b, w_out, g_ffn, w_router, router_bias, w_exp_gate, w_exp_up, w_exp_down, w_sh_gate, w_sh_up, w_sh_down, g_final):
    batch, seq, d = x_prompt.shape
    db, t_new, _ = x_sample.shape
    depth = g_attn.shape[0]
    _, n_pool, page, n_heads, hd = cache_fox_k.shape
    kv_rank = cache_mla_ckv.shape[-1]
    rope = cache_mla_kpe.shape[-1]
    past_len = page_table.shape[1] * page
    mp, ms = batch * seq, db * t_new

    cos_p, sin_p = _rope_tables(jnp.arange(seq, dtype=jnp.int32), rope, 2 * LANES)
    pos_s = past_len + (jnp.arange(ms, dtype=jnp.int32) % t_new)
    cos_s, sin_s = _rope_tables(pos_s, rope, 2 * LANES)

    xp = x_prompt.reshape(mp, d)
    xs = x_sample.reshape(ms, d)
    st_p, st_s = [], []
    for l in range(depth):
        pw = _prep_proj_weights(g_attn[l], w_in[l], b_forget[l], g_q_lat[l], w_uq[l], g_kv_lat[l], w_uk[l])
        qw = _prep_post_weights(w_uv[l], w_branch_a[l], w_branch_b[l], w_out[l], g_ffn[l], w_router[l],
                                router_bias[l], w_sh_gate[l], w_sh_up[l], w_sh_down[l])
        dims = dict(n_heads=n_heads, hd=hd, kv_rank=kv_rank, rope=rope)

        (k_p, v_p, logf_p, _, ckv_p, kpe_p, qf_p, kf_p, vb_p, qcat_p, kcat_p, sga_p, sgb_p) = _proj(
            xp, pw, cos_p, sin_p, tm=PROJ_TM, seq=seq, q_dtype=BF16, **dims)
        oa_p = _fox_attention(qf_p, kf_p, vb_p, batch=batch, seq=seq, tq=FOX_TQ)
        olat_p = _mla_attention(qcat_p, kcat_p, batch=batch, seq=seq, tq=MLA_TQ)

        (k_s, v_s, logf_s, cum_s, ckv_s, kpe_s, qf_s, _, _, qcat_s, _, sga_s, sgb_s) = _proj(
            xs, pw, cos_s, sin_s, tm=PROJ_TM, seq=t_new, q_dtype=F32, **dims)
        kc = jnp.transpose(cache_fox_k[l], (0, 2, 3, 1))
        vc = jnp.transpose(cache_fox_v[l], (0, 2, 3, 1))
        lft = jnp.transpose(cache_fox_logf[l], (0, 2, 1))
        cc = cache_mla_ckv[l]
        pc = jnp.transpose(cache_mla_kpe[l], (0, 2, 1))
        oa_s, olat_s = _sample_attention(page_table, qf_s, k_s, v_s, cum_s, qcat_s, ckv_s, kpe_s, kc, vc, lft, cc, pc,
                                         t_new=t_new, page=page, pages_per_chunk=PAGES_PER_CHUNK)

        x3_p, hn_p, gates_p = _post(xp, oa_p, olat_p, sga_p, sgb_p, qw, tm=POST_TM)
        x3_s, hn_s, gates_s = _post(xs, oa_s, olat_s, sga_s, sgb_s, qw, tm=POST_TM)

        last = l == depth - 1
        g_out = g_final[None, :] if last else None
        assert last, "the final norm is fused into the last layer's MoE kernel; depth > 1 needs an un-normed variant"
        y = _moe(jnp.concatenate([hn_p, hn_s]), jnp.concatenate([gates_p, gates_s]), jnp.concatenate([x3_p, x3_s]),
                 w_exp_gate[l], w_exp_up[l], w_exp_down[l], g_out, tm=(mp + ms) // MOE_TILES)
        xp, xs = y[:mp], y[mp:]
        st_p.append((k_p.reshape(batch, seq, n_heads, hd), v_p.reshape(batch, seq, n_heads, hd),
                     logf_p.reshape(batch, seq, n_heads), ckv_p.reshape(batch, seq, kv_rank),
                     kpe_p.reshape(batch, seq, rope)))
        st_s.append((k_s.reshape(db, t_new, n_heads, hd), v_s.reshape(db, t_new, n_heads, hd),
                     logf_s.reshape(db, t_new, n_heads), ckv_s.reshape(db, t_new, kv_rank),
                     kpe_s.reshape(db, t_new, rope)))

    stack = lambda sts, k: jnp.stack([s[k] for s in sts])
    return (xp.reshape(batch, seq, d), xs.reshape(db, t_new, d),
            *[stack(st_p, k) for k in range(5)], *[stack(st_s, k) for k in range(5)])
```

```python
import functools

import jax
import jax.numpy as jnp
import numpy as np
from jax import lax
from jax.experimental import pallas as pl
from jax.experimental.pallas import tpu as pltpu

F32 = jnp.float32
BF16 = jnp.bfloat16

RMS_EPS = 1e-6
ROPE_BASE = 10000.0
ROUTED_SCALE = 2.5
N_GROUPS = 8
TOPK_GROUPS = 4
TOP_K = 8

LANES = 128
SUBLANES = 8
VMEM_LIMIT = 56 * 1024 * 1024

_C_Q, _C_K, _C_V, _C_CQ, _C_CKV, _C_MISC, _C_GA, _C_GB, _C_END = 0, 512, 1024, 1536, 1792, 1920, 2304, 3328, 4352
AUG = 3


def _nt(a, b):
    return lax.dot_general(a, b, (((1,), (1,)), ((), ())), preferred_element_type=F32)


def _mm(a, b):
    return jnp.dot(a, b, preferred_element_type=F32)


def _rms(x, g):
    return x * lax.rsqrt(jnp.mean(x * x, axis=-1, keepdims=True) + RMS_EPS) * g


def _sigmoid(z):
    return 1.0 / (1.0 + jnp.exp(-z))


def _split3(x):
    hi = x.astype(BF16).astype(F32)
    r1 = x - hi
    mid = r1.astype(BF16).astype(F32)
    lo = r1 - mid
    return hi, mid, lo


def _const_spec(shape):
    nd = len(shape)
    return pl.BlockSpec(shape, lambda *_: (0,) * nd, pipeline_mode=pl.Buffered(1))


def _proj_kernel(x_ref, g_ref, w_ref, b_ref, gq_ref, wuq_ref, gkv_ref, wuk_ref, tri_ref, cos_ref, sin_ref,
                 k32_ref, v32_ref, logf_ref, cum_ref, ckv32_ref, kpe32_ref, qf_ref, kf_ref, vb_ref,
                 qcat_ref, kcat_ref, sga_ref, sgb_ref, carry_ref, *, tiles_per_seq, fox_scale, mla_scale):
    i = pl.program_id(0)
    tm = x_ref.shape[0]
    n_heads = qf_ref.shape[0]
    hb = _rms(x_ref[...], g_ref[...]).astype(BF16)

    def proj(a, b):
        return _mm(hb, w_ref[:, a:b])

    cos = cos_ref[...]
    sin = sin_ref[...]
    lane = lax.broadcasted_iota(jnp.int32, (tm, LANES), 1)

    zm = proj(_C_MISC, _C_GA)
    z = zm[:, :LANES] + b_ref[...]
    logf = jnp.where(lane < n_heads, jnp.minimum(z, 0.0) - jnp.log1p(jnp.exp(-jnp.abs(z))), 0.0)
    logf_ref[...] = logf[:, :n_heads]
    tri = tri_ref[...]
    hi, mid, lo = _split3(logf)
    cum = _mm(tri, hi.astype(BF16)) + _mm(tri, mid.astype(BF16)) + _mm(tri, lo.astype(BF16))
    if tiles_per_seq > 1:
        @pl.when(i % tiles_per_seq == 0)
        def _():
            carry_ref[...] = jnp.zeros_like(carry_ref)
        cum = cum + carry_ref[0:1, :]
        carry_ref[...] = jnp.broadcast_to(cum[tm - 1:tm, :], carry_ref.shape)
    cum_ref[...] = cum[:, :n_heads]

    kpe = zm[:, LANES:2 * LANES] * cos[:, :LANES] + zm[:, 2 * LANES:] * sin[:, :LANES]
    kpe32_ref[...] = kpe[:, :kpe32_ref.shape[1]]
    kcat_ref[:, LANES:] = kpe.astype(BF16)

    zq = proj(_C_Q, _C_K)
    zk = proj(_C_K, _C_V)
    k32_ref[...] = zk
    for h in range(n_heads):
        even = h % 2 == 0
        o = LANES // 2 if even else 0
        data = (lane < LANES // 2) if even else (lane >= LANES // 2)
        c_hi, c_mid, c_lo = _split3(jnp.broadcast_to(cum[:, h:h + 1], (tm, LANES)))
        ones_q = (lane >= o + AUG) & (lane < o + 2 * AUG)
        ones_k = (lane >= o) & (lane < o + AUG)
        aug_q = jnp.where(lane == o, c_hi, jnp.where(lane == o + 1, c_mid, jnp.where(lane == o + 2, c_lo,
                          jnp.where(ones_q, 1.0, 0.0))))
        aug_k = jnp.where(lane == o + AUG, -c_hi, jnp.where(lane == o + AUG + 1, -c_mid,
                          jnp.where(lane == o + AUG + 2, -c_lo, jnp.where(ones_k, 1.0, 0.0))))
        pair = slice((h // 2) * LANES, (h // 2 + 1) * LANES)
        qf_ref[h] = jnp.where(data, zq[:, pair] * fox_scale, aug_q).astype(qf_ref.dtype)
        kf_ref[h] = jnp.where(data, zk[:, pair], aug_k).astype(kf_ref.dtype)

    zv = proj(_C_V, _C_CQ)
    v32_ref[...] = zv
    vb_ref[...] = zv.astype(vb_ref.dtype)

    cqn = _rms(proj(_C_CQ, _C_CKV), gq_ref[...]).astype(BF16)
    qall = _mm(cqn, wuq_ref[...])
    n_nope = qall.shape[1] // 2
    q_nope = qall[:, :n_nope].astype(BF16)
    n_pe = (qall.shape[1] - n_nope) // 2
    q_pe = qall[:, n_nope:n_nope + n_pe] * cos + qall[:, n_nope + n_pe:] * sin
    rope_dim = n_pe // n_heads
    per_blk = LANES // rope_dim
    for j in range(n_heads // 2):
        ql = _mm(q_nope[:, j * LANES:(j + 1) * LANES], wuk_ref[j]) * mla_scale
        for hh in range(2):
            h = 2 * j + hh
            qcat_ref[h, :, :LANES] = ql[:, hh * LANES:(hh + 1) * LANES].astype(qcat_ref.dtype)
            blk = q_pe[:, (h // per_blk) * LANES:(h // per_blk + 1) * LANES]
            lo_l = (h % per_blk) * rope_dim
            keep = (lane >= lo_l) & (lane < lo_l + rope_dim)
            qcat_ref[h, :, LANES:] = (jnp.where(keep, blk, 0.0) * mla_scale).astype(qcat_ref.dtype)

    ckv = _rms(proj(_C_CKV, _C_MISC), gkv_ref[...])
    ckv32_ref[...] = ckv
    kcat_ref[:, :LANES] = ckv.astype(BF16)

    sga_ref[...] = _sigmoid(proj(_C_GA, _C_GB)).astype(sga_ref.dtype)
    sgb_ref[...] = _sigmoid(proj(_C_GB, _C_END)).astype(sgb_ref.dtype)


def _proj(x2d, pw, cos_t, sin_t, *, tm, seq, n_heads, hd, kv_rank, rope, q_dtype):
    m, d = x2d.shape
    tiles_per_seq = max(seq // tm, 1)
    seg = min(seq, tm)
    r = np.arange(tm)
    tri = jnp.asarray((r[None, :] <= r[:, None]) & (r[None, :] // seg == r[:, None] // seg), BF16)
    n_tab = cos_t.shape[0] // tm
    fw = n_heads * hd
    row = lambda w: pl.BlockSpec((tm, w), lambda i: (i, 0))
    head = lambda w: pl.BlockSpec((n_heads, tm, w), lambda i: (0, i, 0))
    out_shape = (
        jax.ShapeDtypeStruct((m, fw), F32), jax.ShapeDtypeStruct((m, fw), F32),
        jax.ShapeDtypeStruct((m, n_heads), F32), jax.ShapeDtypeStruct((m, n_heads), F32),
        jax.ShapeDtypeStruct((m, kv_rank), F32), jax.ShapeDtypeStruct((m, rope), F32),
        jax.ShapeDtypeStruct((n_heads, m, LANES), q_dtype), jax.ShapeDtypeStruct((n_heads, m, LANES), BF16),
        jax.ShapeDtypeStruct((m, fw), BF16),
        jax.ShapeDtypeStruct((n_heads, m, 2 * LANES), q_dtype), jax.ShapeDtypeStruct((m, 2 * LANES), BF16),
        jax.ShapeDtypeStruct((m, d), BF16), jax.ShapeDtypeStruct((m, d), BF16),
    )
    out_specs = (row(fw), row(fw), row(n_heads), row(n_heads), row(kv_rank), row(rope),
                 head(LANES), head(LANES), row(fw), head(2 * LANES), row(2 * LANES), row(d), row(d))
    tab = pl.BlockSpec((tm, cos_t.shape[1]), lambda i: (i % n_tab, 0))
    in_specs = [row(d), _const_spec(pw["g_attn"].shape), _const_spec(pw["w_all"].shape),
                _const_spec(pw["b_pad"].shape), _const_spec(pw["g_q"].shape), _const_spec(pw["w_uq_all"].shape),
                _const_spec(pw["g_kv"].shape), _const_spec(pw["w_uk_bd"].shape), _const_spec(tri.shape), tab, tab]
    body = functools.partial(_proj_kernel, tiles_per_seq=tiles_per_seq, fox_scale=hd ** -0.5,
                             mla_scale=pw["mla_scale"])
    return pl.pallas_call(
        body, out_shape=out_shape, grid=(m // tm,), in_specs=in_specs, out_specs=out_specs,
        scratch_shapes=[pltpu.VMEM((SUBLANES, LANES), F32)],
        compiler_params=pltpu.CompilerParams(dimension_semantics=("arbitrary",), vmem_limit_bytes=VMEM_LIMIT),
        name="proj",
    )(x2d, pw["g_attn"], pw["w_all"], pw["b_pad"], pw["g_q"], pw["w_uq_all"], pw["g_kv"], pw["w_uk_bd"],
      tri, cos_t, sin_t)


def _online(s, m_ref, l_ref, idx):
    rows, cols = s.shape
    m_old = m_ref[idx]
    m_new = jnp.maximum(m_old, jnp.broadcast_to(jnp.max(s, axis=1, keepdims=True), (rows, LANES)))
    corr = jnp.exp(m_old - m_new)
    p = jnp.exp(s - jnp.concatenate([m_new] * (cols // LANES), axis=1))
    m_ref[idx] = m_new
    l_ref[idx] = l_ref[idx] * corr + jnp.broadcast_to(jnp.sum(p, axis=1, keepdims=True), (rows, LANES))
    return p.astype(BF16), corr


def _fox_kernel(qf_ref, kf_ref, v_ref, o_ref, m_ref, l_ref, acc_ref, *, tq):
    seq = v_ref.shape[0]
    row = lax.broadcasted_iota(jnp.int32, (tq, tq), 0)
    col = lax.broadcasted_iota(jnp.int32, (tq, tq), 1)
    lane = lax.broadcasted_iota(jnp.int32, (tq, LANES), 1)

    def q_block(qi, _):
        q0 = pl.multiple_of(qi * tq, tq)
        m_ref[...] = jnp.full(m_ref.shape, -jnp.inf, F32)
        l_ref[...] = jnp.zeros(l_ref.shape, F32)
        acc_ref[...] = jnp.zeros(acc_ref.shape, F32)

        def k_block(ki, masked):
            k0 = pl.multiple_of(ki * tq, tq)
            s = [_nt(qf_ref[hh, pl.ds(q0, tq), :], kf_ref[hh, pl.ds(k0, tq), :]) for hh in range(2)]
            if masked:
                s = [jnp.where(col <= row, x, -jnp.inf) for x in s]
            pc = [_online(s[hh], m_ref, l_ref, hh) for hh in range(2)]
            v = v_ref[pl.ds(k0, tq), :]
            for hh in range(2):
                acc_ref[hh] = acc_ref[hh] * pc[hh][1] + _mm(pc[hh][0], v)

        def body(ki, c):
            k_block(ki, False)
            return c
        lax.fori_loop(0, qi, body, 0)
        k_block(qi, True)
        out = jnp.where(lane < LANES // 2, acc_ref[0] / l_ref[0], acc_ref[1] / l_ref[1])
        o_ref[pl.ds(q0, tq), :] = out.astype(o_ref.dtype)
        return 0
    lax.fori_loop(0, seq // tq, q_block, 0)


def _fox_attention(qf, kf, vb, *, batch, seq, tq):
    n_heads, m, _ = qf.shape
    fw = vb.shape[1]
    return pl.pallas_call(
        functools.partial(_fox_kernel, tq=tq),
        out_shape=jax.ShapeDtypeStruct((m, fw), BF16),
        grid=(batch, n_heads // 2),
        in_specs=[pl.BlockSpec((2, seq, LANES), lambda b, j: (j, b, 0)),
                  pl.BlockSpec((2, seq, LANES), lambda b, j: (j, b, 0)),
                  pl.BlockSpec((seq, LANES), lambda b, j: (b, j))],
        out_specs=pl.BlockSpec((seq, LANES), lambda b, j: (b, j)),
        scratch_shapes=[pltpu.VMEM((2, tq, LANES), F32), pltpu.VMEM((2, tq, LANES), F32),
                        pltpu.VMEM((2, tq, LANES), F32)],
        compiler_params=pltpu.CompilerParams(dimension_semantics=("arbitrary", "arbitrary"),
                                             vmem_limit_bytes=VMEM_LIMIT),
        name="fox_attn",
    )(qf, kf, vb)


def _mla_kernel(q_ref, k_ref, o_ref, m_ref, l_ref, acc_ref, *, tq):
    qi = pl.program_id(1)
    n_heads = q_ref.shape[0]
    rows = n_heads * tq
    q = q_ref[...].reshape(rows, q_ref.shape[2])
    m_ref[...] = jnp.full(m_ref.shape, -jnp.inf, F32)
    l_ref[...] = jnp.zeros(l_ref.shape, F32)
    acc_ref[...] = jnp.zeros(acc_ref.shape, F32)

    def k_block(ki, masked):
        k0 = pl.multiple_of(ki * tq, tq)
        k = k_ref[pl.ds(k0, tq), :]
        s = _nt(q, k)
        if masked:
            tok = lax.broadcasted_iota(jnp.int32, (n_heads, tq, tq), 1).reshape(rows, tq)
            col = lax.broadcasted_iota(jnp.int32, (rows, tq), 1)
            s = jnp.where(col <= tok, s, -jnp.inf)
        p, corr = _online(s, m_ref, l_ref, ...)
        acc_ref[...] = acc_ref[...] * corr + _mm(p, k[:, :LANES])

    def body(ki, c):
        k_block(ki, False)
        return c
    lax.fori_loop(0, qi, body, 0)
    k_block(qi, True)
    o_ref[...] = (acc_ref[...] / l_ref[...]).reshape(o_ref.shape).astype(o_ref.dtype)


def _mla_attention(qcat, kcat, *, batch, seq, tq):
    n_heads, m, w = qcat.shape
    nq = seq // tq
    rows = n_heads * tq
    return pl.pallas_call(
        functools.partial(_mla_kernel, tq=tq),
        out_shape=jax.ShapeDtypeStruct((n_heads, m, LANES), BF16),
        grid=(batch, nq),
        in_specs=[pl.BlockSpec((n_heads, tq, w), lambda b, qi: (0, b * nq + qi, 0)),
                  pl.BlockSpec((seq, w), lambda b, qi: (b, 0))],
        out_specs=pl.BlockSpec((n_heads, tq, LANES), lambda b, qi: (0, b * nq + qi, 0)),
        scratch_shapes=[pltpu.VMEM((rows, LANES), F32), pltpu.VMEM((rows, LANES), F32), pltpu.VMEM((rows, LANES), F32)],
        compiler_params=pltpu.CompilerParams(dimension_semantics=("arbitrary", "arbitrary"),
                                             vmem_limit_bytes=VMEM_LIMIT),
        name="mla_attn",
    )(qcat, kcat)


def _sample_attn_kernel(pt_ref, qf_ref, kn_ref, vn_ref, cum_ref, qcat_ref, ckvn_ref, kpen_ref,
                        kc_ref, vc_ref, lf_ref, cc_ref, pc_ref, tri_ref, oa_ref, olat_ref,
                        kbuf, vbuf, lfbuf, cbuf, pbuf, sem, m_a, l_a, acc_a, m_b, l_b, acc_b, suf_ref,
                        *, pages_per_chunk, page, hd, rope):
    b = pl.program_id(0)
    nb = pl.num_programs(0)
    n_pages = pt_ref.shape[1]
    gpc = pages_per_chunk
    n_chunks = n_pages // gpc
    n_heads, t_new, _ = qf_ref.shape
    rows_b = n_heads * t_new
    half = LANES // 2
    per_blk = LANES // rope

    n_slots = kbuf.shape[0]
    total = nb * n_chunks

    def copies(g, slot):
        bb = g // n_chunks
        c = g % n_chunks
        out = []
        for jj in range(gpc):
            pg = pt_ref[bb, n_pages - (c + 1) * gpc + jj]
            out.append(pltpu.make_async_copy(kc_ref.at[pg], kbuf.at[slot, jj], sem.at[slot, 0]))
            out.append(pltpu.make_async_copy(vc_ref.at[pg], vbuf.at[slot, jj], sem.at[slot, 1]))
            out.append(pltpu.make_async_copy(lf_ref.at[pg], lfbuf.at[slot, jj], sem.at[slot, 2]))
            out.append(pltpu.make_async_copy(cc_ref.at[pg], cbuf.at[slot, jj], sem.at[slot, 3]))
            out.append(pltpu.make_async_copy(pc_ref.at[pg], pbuf.at[slot, jj], sem.at[slot, 4]))
        return out

    def start_all(g, slot):
        for i, cp in enumerate(copies(g, slot)):
            cp.start(priority=i % 2)

    @pl.when(b == 0)
    def _():
        for g0 in range(n_slots - 1):
            start_all(g0, g0)

    q_a = []
    for h in range(n_heads):
        o = 0 if h % 2 == 0 else half
        q_a.append(qf_ref[h][:, o:o + hd].astype(BF16))
    q_lat = qcat_ref[:, :, :LANES].reshape(rows_b, LANES).astype(BF16)
    q_pe = jnp.concatenate(
        [qcat_ref[h][:, LANES + (h % per_blk) * rope:LANES + (h % per_blk + 1) * rope] for h in range(n_heads)],
        axis=0).astype(BF16)
    cn = cum_ref[...]
    cn_col = [jnp.broadcast_to(cn[:, h:h + 1], (t_new, LANES)) for h in range(n_heads)]

    lane_t = lax.broadcasted_iota(jnp.int32, (t_new, LANES), 1)
    row_t = lax.broadcasted_iota(jnp.int32, (t_new, LANES), 0)
    pad_rows = LANES - t_new
    kn = kn_ref[...]
    vn = vn_ref[...]
    pad = lambda x: jnp.concatenate([x, jnp.zeros((pad_rows, x.shape[1]), F32)], axis=0).astype(BF16)
    s_new = [_nt(q_a[h], pad(kn[:, h * hd:(h + 1) * hd])) for h in range(n_heads)]
    ckv_n = pad(ckvn_ref[...])
    lane_b = lax.broadcasted_iota(jnp.int32, (rows_b, LANES), 1)
    tok_b = lax.broadcasted_iota(jnp.int32, (n_heads, t_new, LANES), 1).reshape(rows_b, LANES)
    s_b = jnp.where(lane_b <= tok_b, _nt(q_lat, ckv_n) + _nt(q_pe, pad(kpen_ref[...])), -jnp.inf)
    p_new = []
    for h in range(n_heads):
        cn_row = jnp.sum(jnp.where(lane_t == row_t, cn_col[h], 0.0), axis=0, keepdims=True)
        s = jnp.where(lane_t <= row_t, s_new[h] + cn_col[h] - cn_row, -jnp.inf)
        m = jnp.max(s, axis=1, keepdims=True)
        p = jnp.exp(s - m)
        m_a[h] = m
        l_a[h] = jnp.sum(p, axis=1, keepdims=True)
        p_new.append(p.astype(BF16))
    m = jnp.max(s_b, axis=1, keepdims=True)
    p = jnp.exp(s_b - m)
    m_b[...] = m
    l_b[...] = jnp.sum(p, axis=1, keepdims=True)
    for h in range(n_heads):
        acc_a[h] = _mm(p_new[h], pad(vn[:, h * hd:(h + 1) * hd]))
    acc_b[...] = _mm(p.astype(BF16), ckv_n)
    suf_ref[...] = jnp.zeros_like(suf_ref)

    def online(s, m_ref, l_ref, idx):
        m_old = m_ref[idx]
        m_new = jnp.maximum(m_old, jnp.max(s, axis=1, keepdims=True))
        corr = jnp.exp(m_old - m_new)
        p = jnp.exp(s - m_new)
        m_ref[idx] = m_new
        l_ref[idx] = l_ref[idx] * corr + jnp.sum(p, axis=1, keepdims=True)
        return p.astype(BF16), corr

    def chunk(c, _):
        g = b * n_chunks + c
        slot = g % n_slots
        for cp in copies(g, slot):
            cp.wait()

        @pl.when(g + (n_slots - 1) < total)
        def _():
            start_all(g + (n_slots - 1), (g + (n_slots - 1)) % n_slots)

        s_a = []
        for h in range(n_heads):
            k_t = jnp.concatenate([kbuf[slot, jj, h].astype(BF16) for jj in range(gpc)], axis=1)
            s_a.append(_mm(q_a[h], k_t))
        ck = jnp.concatenate([cbuf[slot, jj].astype(BF16) for jj in range(gpc)], axis=0)
        kp_t = jnp.concatenate([pbuf[slot, jj].astype(BF16) for jj in range(gpc)], axis=1)
        s_b = _nt(q_lat, ck) + _mm(q_pe, kp_t)

        lf = lfbuf[slot]
        rows = gpc * n_heads
        hi, mid, lo = _split3(lf.reshape(rows, page))
        cs = _mm(jnp.concatenate([hi, mid, lo], axis=0).astype(BF16), tri_ref[...])
        cs = cs[:rows] + cs[rows:2 * rows] + cs[2 * rows:]
        excl = cs[:, :page].reshape(gpc, n_heads, page) - lf
        tot = cs[:, page:].reshape(gpc, n_heads, page)
        run = suf_ref[...]
        bias = [None] * gpc
        for jj in reversed(range(gpc)):
            bias[jj] = excl[jj] + run
            run = run + tot[jj]
        suf_ref[...] = run

        p_a, corr_a = [], []
        for h in range(n_heads):
            r_h = jnp.concatenate([jnp.broadcast_to(bias[jj][h:h + 1, :], (t_new, page)) for jj in range(gpc)], axis=1)
            p, corr = online(s_a[h] + cn[:, h:h + 1] + r_h, m_a, l_a, h)
            p_a.append(p)
            corr_a.append(corr)
        p_b, corr_b = online(s_b, m_b, l_b, ...)

        for h in range(n_heads):
            v_t = jnp.concatenate([vbuf[slot, jj, h].astype(BF16) for jj in range(gpc)], axis=1)
            acc_a[h] = acc_a[h] * corr_a[h] + _nt(p_a[h], v_t)
        acc_b[...] = acc_b[...] * corr_b + _mm(p_b, ck)
        return 0

    lax.fori_loop(0, n_chunks, chunk, 0)

    for h in range(n_heads):
        oa_ref[:, h * hd:(h + 1) * hd] = (acc_a[h] / l_a[h]).astype(oa_ref.dtype)
    olat_ref[...] = (acc_b[...] / l_b[...]).reshape(olat_ref.shape).astype(olat_ref.dtype)


def _sample_attention(page_table, qf, k32, v32, cum, qcat, ckv32, kpe32, kc, vc, lft, cc, pc, *,
                      t_new, page, pages_per_chunk):
    n_heads, ms, _ = qf.shape
    fw = k32.shape[1]
    hd = fw // n_heads
    rope = kpe32.shape[1]
    db = ms // t_new
    gpc = pages_per_chunk
    ns = SAMPLE_SLOTS
    assert db * (page_table.shape[1] // gpc) >= ns
    any_spec = pl.BlockSpec(memory_space=pl.ANY)
    row = lambda w: pl.BlockSpec((t_new, w), lambda b, pt: (b, 0))
    head = lambda w: pl.BlockSpec((n_heads, t_new, w), lambda b, pt: (0, b, 0))
    grid_spec = pltpu.PrefetchScalarGridSpec(
        num_scalar_prefetch=1, grid=(db,),
        in_specs=[head(LANES), row(fw), row(fw), row(n_heads), head(2 * LANES), row(LANES), row(rope),
                  any_spec, any_spec, any_spec, any_spec, any_spec,
                  pl.BlockSpec((page, 2 * page), lambda b, pt: (0, 0), pipeline_mode=pl.Buffered(1))],
        out_specs=(row(fw), head(LANES)),
        scratch_shapes=[
            pltpu.VMEM((ns, gpc, n_heads, hd, page), F32), pltpu.VMEM((ns, gpc, n_heads, hd, page), F32),
            pltpu.VMEM((ns, gpc, n_heads, page), F32), pltpu.VMEM((ns, gpc, page, LANES), F32),
            pltpu.VMEM((ns, gpc, rope, page), F32), pltpu.SemaphoreType.DMA((ns, 5)),
            pltpu.VMEM((n_heads, t_new, 1), F32), pltpu.VMEM((n_heads, t_new, 1), F32),
            pltpu.VMEM((n_heads, t_new, hd), F32),
            pltpu.VMEM((n_heads * t_new, 1), F32), pltpu.VMEM((n_heads * t_new, 1), F32),
            pltpu.VMEM((n_heads * t_new, LANES), F32), pltpu.VMEM((n_heads, page), F32),
        ])
    r = np.arange(page)
    tri = jnp.asarray(np.concatenate([r[:, None] >= r[None, :], np.ones((page, page), bool)], axis=1), BF16)
    body = functools.partial(_sample_attn_kernel, pages_per_chunk=gpc, page=page, hd=hd, rope=rope)
    return pl.pallas_call(
        body, grid_spec=grid_spec,
        out_shape=(jax.ShapeDtypeStruct((ms, fw), F32), jax.ShapeDtypeStruct((n_heads, ms, LANES), F32)),
        compiler_params=pltpu.CompilerParams(dimension_semantics=("arbitrary",), vmem_limit_bytes=VMEM_LIMIT),
        name="sample_attn",
    )(page_table, qf, k32, v32, cum, qcat, ckv32, kpe32, kc, vc, lft, cc, pc, tri)


def _route(logits_t, bias_col):
    n_exp, tm = logits_t.shape
    epg = n_exp // N_GROUPS
    score = _sigmoid(logits_t)
    biased = score + bias_col
    b3 = biased.reshape(N_GROUPS, epg, tm)
    sub = lax.broadcasted_iota(jnp.int32, b3.shape, 1)
    m1 = jnp.max(b3, axis=1, keepdims=True)
    i1 = jnp.min(jnp.where(b3 == m1, sub, epg), axis=1, keepdims=True)
    m2 = jnp.max(jnp.where(sub == i1, -jnp.inf, b3), axis=1, keepdims=True)
    g = (m1 + m2).reshape(N_GROUPS, tm)
    gidx = lax.broadcasted_iota(jnp.int32, g.shape, 0)
    gmask = jnp.zeros(g.shape, F32)
    for _ in range(TOPK_GROUPS):
        mg = jnp.max(g, axis=0, keepdims=True)
        ig = jnp.min(jnp.where(g == mg, gidx, N_GROUPS), axis=0, keepdims=True)
        sel = gidx == ig
        gmask = jnp.where(sel, 1.0, gmask)
        g = jnp.where(sel, -jnp.inf, g)
    emask = jnp.concatenate([jnp.broadcast_to(gmask[k:k + 1, :], (epg, tm)) for k in range(N_GROUPS)], axis=0)
    v = jnp.where(emask > 0, biased, -jnp.inf)
    eidx = lax.broadcasted_iota(jnp.int32, v.shape, 0)
    chosen = jnp.zeros(v.shape, F32)
    for _ in range(TOP_K):
        mv = jnp.max(v, axis=0, keepdims=True)
        iv = jnp.min(jnp.where(v == mv, eidx, n_exp), axis=0, keepdims=True)
        sel = eidx == iv
        chosen = jnp.where(sel, 1.0, chosen)
        v = jnp.where(sel, -jnp.inf, v)
    w = chosen * score
    return w / jnp.sum(w, axis=0, keepdims=True) * ROUTED_SCALE


def _post_kernel(x_ref, oa_ref, olat_ref, sga_ref, sgb_ref, wuv_ref, wba_ref, wbb_ref, wout_ref, gffn_ref,
                 wrh_ref, wrl_ref, rbias_ref, wsg_ref, wsu_ref, wsd_ref, x3_ref, hn_ref, gates_ref):
    n_heads = olat_ref.shape[0]
    tm = x_ref.shape[0]
    branch_a = _mm(oa_ref[...].astype(BF16), wba_ref[...])
    ob = jnp.concatenate(
        [_mm(jnp.concatenate([olat_ref[2 * j].astype(BF16), olat_ref[2 * j + 1].astype(BF16)], axis=1), wuv_ref[j])
         for j in range(n_heads // 2)], axis=1).astype(BF16)
    branch_b = _mm(ob, wbb_ref[...])
    merged = sga_ref[...].astype(F32) * branch_a + sgb_ref[...].astype(F32) * branch_b
    x2 = x_ref[...] + _mm(merged.astype(BF16), wout_ref[...])
    hn = _rms(x2, gffn_ref[...])
    hb = hn.astype(BF16)
    hn_ref[...] = hb
    h_lo = (hn - hb.astype(F32)).astype(BF16)
    wrh = wrh_ref[...]
    logits_t = _nt(wrh, hb) + _nt(wrh, h_lo) + _nt(wrl_ref[...], hb)
    gates_t = _route(logits_t, rbias_ref[...])
    n_exp = gates_t.shape[0]
    gates_ref[...] = jnp.concatenate([gates_t, jnp.zeros((LANES - n_exp, tm), F32)], axis=0).T
    sg = _mm(hb, wsg_ref[...])
    su = _mm(hb, wsu_ref[...])
    act = (sg * _sigmoid(sg) * su).astype(BF16)
    x3_ref[...] = x2 + _mm(act, wsd_ref[...])


def _post(x2d, oa, olat, sga, sgb, qw, *, tm):
    m, d = x2d.shape
    n_heads = olat.shape[0]
    row = lambda w: pl.BlockSpec((tm, w), lambda i: (i, 0))
    names = ("w_uv_bd", "w_ba", "w_bb", "w_out", "g_ffn", "w_r_hi", "w_r_lo", "r_bias", "w_sg", "w_su", "w_sd")
    in_specs = [row(d), row(oa.shape[1]), pl.BlockSpec((n_heads, tm, LANES), lambda i: (0, i, 0)), row(d), row(d)]
    in_specs += [_const_spec(qw[k].shape) for k in names]
    return pl.pallas_call(
        _post_kernel,
        out_shape=(jax.ShapeDtypeStruct((m, d), F32), jax.ShapeDtypeStruct((m, d), BF16),
                   jax.ShapeDtypeStruct((m, LANES), F32)),
        grid=(m // tm,), in_specs=in_specs, out_specs=(row(d), row(d), row(LANES)),
        compiler_params=pltpu.CompilerParams(dimension_semantics=("arbitrary",), vmem_limit_bytes=VMEM_LIMIT),
        name="post",
    )(x2d, oa, olat, sga, sgb, *[qw[k] for k in names])


def _moe_kernel(hn_ref, gates_ref, x3_ref, wg_ref, wu_ref, wd_ref, gfin_ref, y_ref, *, n_split):
    e = pl.program_id(1)
    rows = hn_ref.shape[0] // n_split

    @pl.when(e == 0)
    def _():
        y_ref[...] = x3_ref[...]

    wg = wg_ref[0].astype(BF16)
    wu = wu_ref[0].astype(BF16)
    wd = wd_ref[0].astype(BF16)
    parts = []
    for r in range(n_split):
        hb = hn_ref[r * rows:(r + 1) * rows, :]
        parts.append((_mm(hb, wg), _mm(hb, wu)))
    for r, (hg, hu) in enumerate(parts):
        gates = gates_ref[r * rows:(r + 1) * rows, :]
        lane = lax.broadcasted_iota(jnp.int32, gates.shape, 1)
        gate = jnp.sum(jnp.where(lane == e, gates, 0.0), axis=1, keepdims=True)
        act = (hg * _sigmoid(hg) * hu * gate).astype(BF16)
        y_ref[r * rows:(r + 1) * rows, :] += _mm(act, wd)

    @pl.when(e == pl.num_programs(1) - 1)
    def _():
        y_ref[...] = _rms(y_ref[...], gfin_ref[...])


def _moe(hn, gates, x3, w_gate, w_up, w_down, g_final, *, tm):
    m, d = x3.shape
    n_exp, _, ff = w_gate.shape
    once = lambda w: pl.BlockSpec((tm, w), lambda i, e: (i, 0), pipeline_mode=pl.Buffered(1))
    return pl.pallas_call(
        functools.partial(_moe_kernel, n_split=MOE_ROW_SPLIT),
        out_shape=jax.ShapeDtypeStruct((m, d), F32),
        grid=(m // tm, n_exp),
        in_specs=[once(d), once(LANES), once(d),
                  pl.BlockSpec((1, d, ff), lambda i, e: (e, 0, 0)), pl.BlockSpec((1, d, ff), lambda i, e: (e, 0, 0)),
                  pl.BlockSpec((1, ff, d), lambda i, e: (e, 0, 0)), _const_spec(g_final.shape)],
        out_specs=pl.BlockSpec((tm, d), lambda i, e: (i, 0)),
        compiler_params=pltpu.CompilerParams(dimension_semantics=("arbitrary", "arbitrary"),
                                             vmem_limit_bytes=VMEM_LIMIT),
        name="moe",
    )(hn, gates, x3, w_gate, w_up, w_down, g_final)


def _prep_proj_weights(g_attn, w_in, b_forget, g_q_lat, w_uq, g_kv_lat, w_uk):
    q_rank, n_heads, qk = w_uq.shape
    kv_rank, _, nope = w_uk.shape
    rope = qk - nope
    half = rope // 2
    fw = (w_in.shape[1] - n_heads - q_rank - kv_rank - rope - 2 * w_in.shape[0]) // 3
    sizes = (fw, fw, fw, n_heads, q_rank, kv_rank, rope, w_in.shape[0], w_in.shape[0])
    qa, ka, va, fa, cq, ckv, kr, ga, gb = jnp.split(w_in, [int(v) for v in np.cumsum(sizes)[:-1]], axis=1)
    kr_sw = jnp.concatenate([-kr[:, half:], kr[:, :half]], axis=1)
    rep = LANES // rope
    misc = jnp.concatenate([fa, jnp.zeros((w_in.shape[0], LANES - n_heads), F32),
                            jnp.tile(kr, (1, rep)), jnp.tile(kr_sw, (1, rep))], axis=1)
    w_all = jnp.concatenate([qa, ka, va, cq, ckv, misc, ga, gb], axis=1).astype(BF16)
    assert w_all.shape[1] == _C_END, w_all.shape
    w_pe = w_uq[:, :, nope:]
    w_pe_sw = jnp.concatenate([-w_pe[..., half:], w_pe[..., :half]], axis=-1)
    w_uq_all = jnp.concatenate([w_uq[:, :, :nope].reshape(q_rank, -1), w_pe.reshape(q_rank, -1),
                                w_pe_sw.reshape(q_rank, -1)], axis=1).astype(BF16)
    wk = jnp.transpose(w_uk, (1, 2, 0))
    z = jnp.zeros_like(wk[0])
    w_uk_bd = jnp.stack([jnp.concatenate([jnp.concatenate([wk[2 * j], z], axis=1),
                                          jnp.concatenate([z, wk[2 * j + 1]], axis=1)], axis=0)
                         for j in range(n_heads // 2)]).astype(BF16)
    b_pad = jnp.concatenate([b_forget, jnp.zeros((LANES - n_heads,), F32)])[None, :]
    return dict(g_attn=g_attn[None, :], w_all=w_all, b_pad=b_pad, g_q=g_q_lat[None, :], w_uq_all=w_uq_all,
                g_kv=g_kv_lat[None, :], w_uk_bd=w_uk_bd, mla_scale=float(qk) ** -0.5)


def _prep_post_weights(w_uv, w_branch_a, w_branch_b, w_out, g_ffn, w_router, router_bias, w_sh_gate, w_sh_up,
                       w_sh_down):
    n_heads = w_uv.shape[1]
    wv = jnp.transpose(w_uv, (1, 0, 2))
    z = jnp.zeros_like(wv[0])
    w_uv_bd = jnp.stack([jnp.concatenate([jnp.concatenate([wv[2 * j], z], axis=1),
                                          jnp.concatenate([z, wv[2 * j + 1]], axis=1)], axis=0)
                         for j in range(n_heads // 2)]).astype(BF16)
    wr_t = w_router.T
    wr_hi = wr_t.astype(BF16)
    wr_lo = (wr_t - wr_hi.astype(F32)).astype(BF16)
    return dict(w_uv_bd=w_uv_bd, w_ba=w_branch_a.astype(BF16), w_bb=w_branch_b.astype(BF16),
                w_out=w_out.astype(BF16), g_ffn=g_ffn[None, :], w_r_hi=wr_hi, w_r_lo=wr_lo,
                r_bias=router_bias[:, None], w_sg=w_sh_gate.astype(BF16), w_su=w_sh_up.astype(BF16),
                w_sd=w_sh_down.astype(BF16))


def _rope_tables(pos, rope, width):
    half = rope // 2
    inv_freq = ROPE_BASE ** (-jnp.arange(half, dtype=F32) / half)
    ang = pos.astype(F32)[:, None] * inv_freq[None, :]
    cos = jnp.cos(ang)
    sin = jnp.sin(ang)
    reps = width // rope
    return jnp.tile(jnp.concatenate([cos, cos], axis=1), (1, reps)), jnp.tile(jnp.concatenate([sin, sin], axis=1), (1, reps))


PROJ_TM = 512
FOX_TQ = 512
MLA_TQ = 256
POST_TM = 512
MOE_TILES = 8
MOE_ROW_SPLIT = 4
PAGES_PER_CHUNK = 8
SAMPLE_SLOTS = 3


def kernel(x_prompt, x_sample, cache_fox_k, cache_fox_v, cache_fox_logf, cache_mla_ckv, cache_mla_kpe, page_table, g_attn, w_in, b_forget, g_q_lat, w_uq, g_kv_lat, w_uk, w_uv, w_branch_a, w_branch_b, w_out, g_ffn, w_router, router_bias, w_exp_gate, w_exp_up, w_exp_down, w_sh_gate, w_sh_up, w_sh_down, g_final):
    batch, seq, d = x_prompt.shape
    db, t_new, _ = x_sample.shape
    depth = g_attn.shape[0]
    _, n_pool, page, n_heads, hd = cache_fox_k.shape
    kv_rank = cache_mla_ckv.shape[-1]
    rope = cache_mla_kpe.shape[-1]
    past_len = page_table.shape[1] * page
    mp, ms = batch * seq, db * t_new

    cos_p, sin_p = _rope_tables(jnp.arange(seq, dtype=jnp.int32), rope, 2 * LANES)
    pos_s = past_len + (jnp.arange(ms, dtype=jnp.int32) % t_new)
    cos_s, sin_s = _rope_tables(pos_s, rope, 2 * LANES)

    xp = x_prompt.reshape(mp, d)
    xs = x_sample.reshape(ms, d)
    st_p, st_s = [], []
    for l in range(depth):
        pw = _prep_proj_weights(g_attn[l], w_in[l], b_forget[l], g_q_lat[l], w_uq[l], g_kv_lat[l], w_uk[l])
        qw = _prep_post_weights(w_uv[l], w_branch_a[l], w_branch_b[l], w_out[l], g_ffn[l], w_router[l],
                                router_bias[l], w_sh_gate[l], w_sh_up[l], w_sh_down[l])
        dims = dict(n_heads=n_heads, hd=hd, kv_rank=kv_rank, rope=rope)

        (k_p, v_p, logf_p, _, ckv_p, kpe_p, qf_p, kf_p, vb_p, qcat_p, kcat_p, sga_p, sgb_p) = _proj(
            xp, pw, cos_p, sin_p, tm=PROJ_TM, seq=seq, q_dtype=BF16, **dims)
        oa_p = _fox_attention(qf_p, kf_p, vb_p, batch=batch, seq=seq, tq=FOX_TQ)
        olat_p = _mla_attention(qcat_p, kcat_p, batch=batch, seq=seq, tq=MLA_TQ)

        (k_s, v_s, logf_s, cum_s, ckv_s, kpe_s, qf_s, _, _, qcat_s, _, sga_s, sgb_s) = _proj(
            xs, pw, cos_s, sin_s, tm=PROJ_TM, seq=t_new, q_dtype=F32, **dims)
        kc = jnp.transpose(cache_fox_k[l], (0, 2, 3, 1))
        vc = jnp.transpose(cache_fox_v[l], (0, 2, 3, 1))
        lft = jnp.transpose(cache_fox_logf[l], (0, 2, 1))
        cc = cache_mla_ckv[l]
        pc = jnp.transpose(cache_mla_kpe[l], (0, 2, 1))
        oa_s, olat_s = _sample_attention(page_table, qf_s, k_s, v_s, cum_s, qcat_s, ckv_s, kpe_s, kc, vc, lft, cc, pc,
                                         t_new=t_new, page=page, pages_per_chunk=PAGES_PER_CHUNK)

        x3_p, hn_p, gates_p = _post(xp, oa_p, olat_p, sga_p, sgb_p, qw, tm=POST_TM)
        x3_s, hn_s, gates_s = _post(xs, oa_s, olat_s, sga_s, sgb_s, qw, tm=POST_TM)

        last = l == depth - 1
        g_out = g_final[None, :] if last else None
        assert last, "the final norm is fused into the last layer's MoE kernel; depth > 1 needs an un-normed variant"
        y = _moe(jnp.concatenate([hn_p, hn_s]), jnp.concatenate([gates_p, gates_s]), jnp.concatenate([x3_p, x3_s]),
                 w_exp_gate[l], w_exp_up[l], w_exp_down[l], g_out, tm=(mp + ms) // MOE_TILES)
        xp, xs = y[:mp], y[mp:]
        st_p.append((k_p.reshape(batch, seq, n_heads, hd), v_p.reshape(batch, seq, n_heads, hd),
                     logf_p.reshape(batch, seq, n_heads), ckv_p.reshape(batch, seq, kv_rank),
                     kpe_p.reshape(batch, seq, rope)))
        st_s.append((k_s.reshape(db, t_new, n_heads, hd), v_s.reshape(db, t_new, n_heads, hd),
                     logf_s.reshape(db, t_new, n_heads), ckv_s.reshape(db, t_new, kv_rank),
                     kpe_s.reshape(db, t_new, rope)))

    stack = lambda sts, k: jnp.stack([s[k] for s in sts])
    return (xp.reshape(batch, seq, d), xs.reshape(db, t_new, d),
            *[stack(st_p, k) for k in range(5)], *[stack(st_s, k) for k in range(5)])
```

```python
import functools

import jax
import jax.numpy as jnp
import numpy as np
from jax import lax
from jax.experimental import pallas as pl
from jax.experimental.pallas import tpu as pltpu

F32 = jnp.float32
BF16 = jnp.bfloat16

RMS_EPS = 1e-6
ROPE_BASE = 10000.0
ROUTED_SCALE = 2.5
N_GROUPS = 8
TOPK_GROUPS = 4
TOP_K = 8

LANES = 128
SUBLANES = 8
VMEM_LIMIT = 56 * 1024 * 1024

_C_Q, _C_K, _C_V, _C_CQ, _C_CKV, _C_MISC, _C_GA, _C_GB, _C_END = 0, 512, 1024, 1536, 1792, 1920, 2304, 3328, 4352
AUG = 3


def _nt(a, b):
    return lax.dot_general(a, b, (((1,), (1,)), ((), ())), preferred_element_type=F32)


def _mm(a, b):
    return jnp.dot(a, b, preferred_element_type=F32)


def _rms(x, g):
    return x * lax.rsqrt(jnp.mean(x * x, axis=-1, keepdims=True) + RMS_EPS) * g


def _sigmoid(z):
    return 1.0 / (1.0 + jnp.exp(-z))


def _split3(x):
    hi = x.astype(BF16).astype(F32)
    r1 = x - hi
    mid = r1.astype(BF16).astype(F32)
    lo = r1 - mid
    return hi, mid, lo


def _const_spec(shape):
    nd = len(shape)
    return pl.BlockSpec(shape, lambda *_: (0,) * nd, pipeline_mode=pl.Buffered(1))


def _proj_kernel(x_ref, g_ref, w_ref, b_ref, gq_ref, wuq_ref, gkv_ref, wuk_ref, tri_ref, cos_ref, sin_ref,
                 k32_ref, v32_ref, logf_ref, cum_ref, ckv32_ref, kpe32_ref, qf_ref, kf_ref, vb_ref,
                 qcat_ref, kcat_ref, sga_ref, sgb_ref, carry_ref, *, tiles_per_seq, fox_scale, mla_scale, kv_transposed):
    i = pl.program_id(0)
    tm = x_ref.shape[0]
    n_heads = qf_ref.shape[0]
    hb = _rms(x_ref[...], g_ref[...]).astype(BF16)

    def proj(a, b):
        return _mm(hb, w_ref[:, a:b])

    cos = cos_ref[...]
    sin = sin_ref[...]
    lane = lax.broadcasted_iota(jnp.int32, (tm, LANES), 1)

    zm = proj(_C_MISC, _C_GA)
    z = zm[:, :LANES] + b_ref[...]
    logf = jnp.where(lane < n_heads, jnp.minimum(z, 0.0) - jnp.log1p(jnp.exp(-jnp.abs(z))), 0.0)
    logf_ref[...] = logf[:, :n_heads]
    tri = tri_ref[...]
    hi, mid, lo = _split3(logf)
    cum = _mm(tri, hi.astype(BF16)) + _mm(tri, mid.astype(BF16)) + _mm(tri, lo.astype(BF16))
    if tiles_per_seq > 1:
        @pl.when(i % tiles_per_seq == 0)
        def _():
            carry_ref[...] = jnp.zeros_like(carry_ref)
        cum = cum + carry_ref[0:1, :]
        carry_ref[...] = jnp.broadcast_to(cum[tm - 1:tm, :], carry_ref.shape)
    cum_ref[...] = cum[:, :n_heads]

    kpe = zm[:, LANES:2 * LANES] * cos[:, :LANES] + zm[:, 2 * LANES:] * sin[:, :LANES]
    kpe32_ref[...] = kpe[:, :kpe32_ref.shape[1]]
    kcat_ref[:, LANES:] = kpe.astype(BF16)

    zq = proj(_C_Q, _C_K)
    zk = proj(_C_K, _C_V)
    if kv_transposed:
        k32_ref[0] = zk.T
    else:
        k32_ref[...] = zk
    for h in range(n_heads):
        even = h % 2 == 0
        o = LANES // 2 if even else 0
        data = (lane < LANES // 2) if even else (lane >= LANES // 2)
        c_hi, c_mid, c_lo = _split3(jnp.broadcast_to(cum[:, h:h + 1], (tm, LANES)))
        ones_q = (lane >= o + AUG) & (lane < o + 2 * AUG)
        ones_k = (lane >= o) & (lane < o + AUG)
        aug_q = jnp.where(lane == o, c_hi, jnp.where(lane == o + 1, c_mid, jnp.where(lane == o + 2, c_lo,
                          jnp.where(ones_q, 1.0, 0.0))))
        aug_k = jnp.where(lane == o + AUG, -c_hi, jnp.where(lane == o + AUG + 1, -c_mid,
                          jnp.where(lane == o + AUG + 2, -c_lo, jnp.where(ones_k, 1.0, 0.0))))
        pair = slice((h // 2) * LANES, (h // 2 + 1) * LANES)
        qf_ref[h] = jnp.where(data, zq[:, pair] * fox_scale, aug_q).astype(qf_ref.dtype)
        kf_ref[h] = jnp.where(data, zk[:, pair], aug_k).astype(kf_ref.dtype)

    zv = proj(_C_V, _C_CQ)
    if kv_transposed:
        v32_ref[0] = zv.T
    else:
        v32_ref[...] = zv
    vb_ref[...] = zv.astype(vb_ref.dtype)

    cqn = _rms(proj(_C_CQ, _C_CKV), gq_ref[...]).astype(BF16)
    qall = _mm(cqn, wuq_ref[...])
    n_nope = qall.shape[1] // 2
    q_nope = qall[:, :n_nope].astype(BF16)
    n_pe = (qall.shape[1] - n_nope) // 2
    q_pe = qall[:, n_nope:n_nope + n_pe] * cos + qall[:, n_nope + n_pe:] * sin
    rope_dim = n_pe // n_heads
    per_blk = LANES // rope_dim
    for j in range(n_heads // 2):
        ql = _mm(q_nope[:, j * LANES:(j + 1) * LANES], wuk_ref[j]) * mla_scale
        for hh in range(2):
            h = 2 * j + hh
            qcat_ref[h, :, :LANES] = ql[:, hh * LANES:(hh + 1) * LANES].astype(qcat_ref.dtype)
            blk = q_pe[:, (h // per_blk) * LANES:(h // per_blk + 1) * LANES]
            lo_l = (h % per_blk) * rope_dim
            keep = (lane >= lo_l) & (lane < lo_l + rope_dim)
            qcat_ref[h, :, LANES:] = (jnp.where(keep, blk, 0.0) * mla_scale).astype(qcat_ref.dtype)

    ckv = _rms(proj(_C_CKV, _C_MISC), gkv_ref[...])
    ckv32_ref[...] = ckv
    kcat_ref[:, :LANES] = ckv.astype(BF16)

    sga_ref[...] = _sigmoid(proj(_C_GA, _C_GB)).astype(sga_ref.dtype)
    sgb_ref[...] = _sigmoid(proj(_C_GB, _C_END)).astype(sgb_ref.dtype)


def _proj(x2d, pw, cos_t, sin_t, *, tm, seq, n_heads, hd, kv_rank, rope, q_dtype, kv_transposed):
    m, d = x2d.shape
    tiles_per_seq = max(seq // tm, 1)
    seg = min(seq, tm)
    r = np.arange(tm)
    tri = jnp.asarray((r[None, :] <= r[:, None]) & (r[None, :] // seg == r[:, None] // seg), BF16)
    n_tab = cos_t.shape[0] // tm
    fw = n_heads * hd
    row = lambda w: pl.BlockSpec((tm, w), lambda i: (i, 0))
    head = lambda w: pl.BlockSpec((n_heads, tm, w), lambda i: (0, i, 0))
    if kv_transposed:
        kv_shape = jax.ShapeDtypeStruct((m // seq, fw, seq), F32)
        kv_spec = pl.BlockSpec((1, fw, tm), lambda i: (i // tiles_per_seq, 0, i % tiles_per_seq))
    else:
        kv_shape = jax.ShapeDtypeStruct((m, fw), F32)
        kv_spec = row(fw)
    out_shape = (
        kv_shape, kv_shape,
        jax.ShapeDtypeStruct((m, n_heads), F32), jax.ShapeDtypeStruct((m, n_heads), F32),
        jax.ShapeDtypeStruct((m, kv_rank), F32), jax.ShapeDtypeStruct((m, rope), F32),
        jax.ShapeDtypeStruct((n_heads, m, LANES), q_dtype), jax.ShapeDtypeStruct((n_heads, m, LANES), BF16),
        jax.ShapeDtypeStruct((m, fw), BF16),
        jax.ShapeDtypeStruct((n_heads, m, 2 * LANES), q_dtype), jax.ShapeDtypeStruct((m, 2 * LANES), BF16),
        jax.ShapeDtypeStruct((m, d), BF16), jax.ShapeDtypeStruct((m, d), BF16),
    )
    out_specs = (kv_spec, kv_spec, row(n_heads), row(n_heads), row(kv_rank), row(rope),
                 head(LANES), head(LANES), row(fw), head(2 * LANES), row(2 * LANES), row(d), row(d))
    tab = pl.BlockSpec((tm, cos_t.shape[1]), lambda i: (i % n_tab, 0))
    in_specs = [row(d), _const_spec(pw["g_attn"].shape), _const_spec(pw["w_all"].shape),
                _const_spec(pw["b_pad"].shape), _const_spec(pw["g_q"].shape), _const_spec(pw["w_uq_all"].shape),
                _const_spec(pw["g_kv"].shape), _const_spec(pw["w_uk_bd"].shape), _const_spec(tri.shape), tab, tab]
    body = functools.partial(_proj_kernel, tiles_per_seq=tiles_per_seq, fox_scale=hd ** -0.5,
                             mla_scale=pw["mla_scale"], kv_transposed=kv_transposed)
    return pl.pallas_call(
        body, out_shape=out_shape, grid=(m // tm,), in_specs=in_specs, out_specs=out_specs,
        scratch_shapes=[pltpu.VMEM((SUBLANES, LANES), F32)],
        compiler_params=pltpu.CompilerParams(dimension_semantics=("arbitrary",), vmem_limit_bytes=VMEM_LIMIT),
        name="proj",
    )(x2d, pw["g_attn"], pw["w_all"], pw["b_pad"], pw["g_q"], pw["w_uq_all"], pw["g_kv"], pw["w_uk_bd"],
      tri, cos_t, sin_t)


def _online(s, m_ref, l_ref, idx):
    rows, cols = s.shape
    m_old = m_ref[idx]
    m_new = jnp.maximum(m_old, jnp.broadcast_to(jnp.max(s, axis=1, keepdims=True), (rows, LANES)))
    corr = jnp.exp(m_old - m_new)
    p = jnp.exp(s - jnp.concatenate([m_new] * (cols // LANES), axis=1))
    m_ref[idx] = m_new
    l_ref[idx] = l_ref[idx] * corr + jnp.broadcast_to(jnp.sum(p, axis=1, keepdims=True), (rows, LANES))
    return p.astype(BF16), corr


def _fox_kernel(qf_ref, kf_ref, v_ref, o_ref, m_ref, l_ref, acc_ref, *, tq):
    seq = v_ref.shape[0]
    row = lax.broadcasted_iota(jnp.int32, (tq, tq), 0)
    col = lax.broadcasted_iota(jnp.int32, (tq, tq), 1)
    lane = lax.broadcasted_iota(jnp.int32, (tq, LANES), 1)

    def q_block(qi, _):
        q0 = pl.multiple_of(qi * tq, tq)
        m_ref[...] = jnp.full(m_ref.shape, -jnp.inf, F32)
        l_ref[...] = jnp.zeros(l_ref.shape, F32)
        acc_ref[...] = jnp.zeros(acc_ref.shape, F32)

        def k_block(ki, masked):
            k0 = pl.multiple_of(ki * tq, tq)
            s = [_nt(qf_ref[hh, pl.ds(q0, tq), :], kf_ref[hh, pl.ds(k0, tq), :]) for hh in range(2)]
            if masked:
                s = [jnp.where(col <= row, x, -jnp.inf) for x in s]
            pc = [_online(s[hh], m_ref, l_ref, hh) for hh in range(2)]
            v = v_ref[pl.ds(k0, tq), :]
            for hh in range(2):
                acc_ref[hh] = acc_ref[hh] * pc[hh][1] + _mm(pc[hh][0], v)

        def body(ki, c):
            k_block(ki, False)
            return c
        lax.fori_loop(0, qi, body, 0)
        k_block(qi, True)
        out = jnp.where(lane < LANES // 2, acc_ref[0] / l_ref[0], acc_ref[1] / l_ref[1])
        o_ref[pl.ds(q0, tq), :] = out.astype(o_ref.dtype)
        return 0
    lax.fori_loop(0, seq // tq, q_block, 0)


def _fox_attention(qf, kf, vb, *, batch, seq, tq):
    n_heads, m, _ = qf.shape
    fw = vb.shape[1]
    return pl.pallas_call(
        functools.partial(_fox_kernel, tq=tq),
        out_shape=jax.ShapeDtypeStruct((m, fw), BF16),
        grid=(batch, n_heads // 2),
        in_specs=[pl.BlockSpec((2, seq, LANES), lambda b, j: (j, b, 0)),
                  pl.BlockSpec((2, seq, LANES), lambda b, j: (j, b, 0)),
                  pl.BlockSpec((seq, LANES), lambda b, j: (b, j))],
        out_specs=pl.BlockSpec((seq, LANES), lambda b, j: (b, j)),
        scratch_shapes=[pltpu.VMEM((2, tq, LANES), F32), pltpu.VMEM((2, tq, LANES), F32),
                        pltpu.VMEM((2, tq, LANES), F32)],
        compiler_params=pltpu.CompilerParams(dimension_semantics=("arbitrary", "arbitrary"),
                                             vmem_limit_bytes=VMEM_LIMIT),
        name="fox_attn",
    )(qf, kf, vb)


def _mla_kernel(q_ref, k_ref, o_ref, m_ref, l_ref, acc_ref, *, tq):
    qi = pl.program_id(1)
    n_heads = q_ref.shape[0]
    rows = n_heads * tq
    q = q_ref[...].reshape(rows, q_ref.shape[2])
    m_ref[...] = jnp.full(m_ref.shape, -jnp.inf, F32)
    l_ref[...] = jnp.zeros(l_ref.shape, F32)
    acc_ref[...] = jnp.zeros(acc_ref.shape, F32)

    def k_block(ki, masked):
        k0 = pl.multiple_of(ki * tq, tq)
        k = k_ref[pl.ds(k0, tq), :]
        s = _nt(q, k)
        if masked:
            tok = lax.broadcasted_iota(jnp.int32, (n_heads, tq, tq), 1).reshape(rows, tq)
            col = lax.broadcasted_iota(jnp.int32, (rows, tq), 1)
            s = jnp.where(col <= tok, s, -jnp.inf)
        p, corr = _online(s, m_ref, l_ref, ...)
        acc_ref[...] = acc_ref[...] * corr + _mm(p, k[:, :LANES])

    def body(ki, c):
        k_block(ki, False)
        return c
    lax.fori_loop(0, qi, body, 0)
    k_block(qi, True)
    o_ref[...] = (acc_ref[...] / l_ref[...]).reshape(o_ref.shape).astype(o_ref.dtype)


def _mla_attention(qcat, kcat, *, batch, seq, tq):
    n_heads, m, w = qcat.shape
    nq = seq // tq
    rows = n_heads * tq
    return pl.pallas_call(
        functools.partial(_mla_kernel, tq=tq),
        out_shape=jax.ShapeDtypeStruct((n_heads, m, LANES), BF16),
        grid=(batch, nq),
        in_specs=[pl.BlockSpec((n_heads, tq, w), lambda b, qi: (0, b * nq + qi, 0)),
                  pl.BlockSpec((seq, w), lambda b, qi: (b, 0))],
        out_specs=pl.BlockSpec((n_heads, tq, LANES), lambda b, qi: (0, b * nq + qi, 0)),
        scratch_shapes=[pltpu.VMEM((rows, LANES), F32), pltpu.VMEM((rows, LANES), F32), pltpu.VMEM((rows, LANES), F32)],
        compiler_params=pltpu.CompilerParams(dimension_semantics=("arbitrary", "arbitrary"),
                                             vmem_limit_bytes=VMEM_LIMIT),
        name="mla_attn",
    )(qcat, kcat)


def _sample_attn_kernel(pt_ref, qf_ref, kn_ref, vn_ref, cum_ref, qcat_ref, ckvn_ref, kpen_ref,
                        kc_ref, vc_ref, lf_ref, cc_ref, pc_ref, tri_ref, oa_ref, olat_ref,
                        kbuf, vbuf, lfbuf, cbuf, pbuf, sem, m_a, l_a, acc_a, m_b, l_b, acc_b, suf_ref,
                        *, pages_per_chunk, page, hd, rope):
    b = pl.program_id(0)
    nb = pl.num_programs(0)
    n_pages = pt_ref.shape[1]
    gpc = pages_per_chunk
    n_chunks = n_pages // gpc
    n_heads, t_new, _ = qf_ref.shape
    rows_b = n_heads * t_new
    half = LANES // 2
    per_blk = LANES // rope

    n_slots = kbuf.shape[0]
    total = nb * n_chunks

    def copies(g, slot):
        bb = g // n_chunks
        c = g % n_chunks
        out = []
        for jj in range(gpc):
            pg = pt_ref[bb, n_pages - (c + 1) * gpc + jj]
            out.append(pltpu.make_async_copy(kc_ref.at[pg], kbuf.at[slot, jj], sem.at[slot, 0]))
            out.append(pltpu.make_async_copy(vc_ref.at[pg], vbuf.at[slot, jj], sem.at[slot, 1]))
            out.append(pltpu.make_async_copy(lf_ref.at[pg], lfbuf.at[slot, jj], sem.at[slot, 2]))
            out.append(pltpu.make_async_copy(cc_ref.at[pg], cbuf.at[slot, jj], sem.at[slot, 3]))
            out.append(pltpu.make_async_copy(pc_ref.at[pg], pbuf.at[slot, jj], sem.at[slot, 4]))
        return out

    def start_all(g, slot):
        for i, cp in enumerate(copies(g, slot)):
            cp.start(priority=i % 2)

    @pl.when(b == 0)
    def _():
        for g0 in range(n_slots - 1):
            start_all(g0, g0)

    q_a = []
    for h in range(n_heads):
        o = 0 if h % 2 == 0 else half
        q_a.append(qf_ref[h][:, o:o + hd].astype(BF16))
    q_lat = qcat_ref[:, :, :LANES].reshape(rows_b, LANES).astype(BF16)
    q_pe = jnp.concatenate(
        [qcat_ref[h][:, LANES + (h % per_blk) * rope:LANES + (h % per_blk + 1) * rope] for h in range(n_heads)],
        axis=0).astype(BF16)
    cn = cum_ref[...]
    cn_col = [jnp.broadcast_to(cn[:, h:h + 1], (t_new, LANES)) for h in range(n_heads)]

    lane_t = lax.broadcasted_iota(jnp.int32, (t_new, LANES), 1)
    row_t = lax.broadcasted_iota(jnp.int32, (t_new, LANES), 0)
    pad_rows = LANES - t_new
    kn = kn_ref[...]
    vn = vn_ref[...]
    pad = lambda x: jnp.concatenate([x, jnp.zeros((pad_rows, x.shape[1]), F32)], axis=0).astype(BF16)
    s_new = [_nt(q_a[h], pad(kn[:, h * hd:(h + 1) * hd])) for h in range(n_heads)]
    ckv_n = pad(ckvn_ref[...])
    lane_b = lax.broadcasted_iota(jnp.int32, (rows_b, LANES), 1)
    tok_b = lax.broadcasted_iota(jnp.int32, (n_heads, t_new, LANES), 1).reshape(rows_b, LANES)
    s_b = jnp.where(lane_b <= tok_b, _nt(q_lat, ckv_n) + _nt(q_pe, pad(kpen_ref[...])), -jnp.inf)
    p_new = []
    for h in range(n_heads):
        cn_row = jnp.sum(jnp.where(lane_t == row_t, cn_col[h], 0.0), axis=0, keepdims=True)
        s = jnp.where(lane_t <= row_t, s_new[h] + cn_col[h] - cn_row, -jnp.inf)
        m = jnp.max(s, axis=1, keepdims=True)
        p = jnp.exp(s - m)
        m_a[h] = m
        l_a[h] = jnp.sum(p, axis=1, keepdims=True)
        p_new.append(p.astype(BF16))
    m = jnp.max(s_b, axis=1, keepdims=True)
    p = jnp.exp(s_b - m)
    m_b[...] = m
    l_b[...] = jnp.sum(p, axis=1, keepdims=True)
    for h in range(n_heads):
        acc_a[h] = _mm(p_new[h], pad(vn[:, h * hd:(h + 1) * hd]))
    acc_b[...] = _mm(p.astype(BF16), ckv_n)
    suf_ref[...] = jnp.zeros_like(suf_ref)

    def online(s, m_ref, l_ref, idx):
        m_old = m_ref[idx]
        m_new = jnp.maximum(m_old, jnp.max(s, axis=1, keepdims=True))
        corr = jnp.exp(m_old - m_new)
        p = jnp.exp(s - m_new)
        m_ref[idx] = m_new
        l_ref[idx] = l_ref[idx] * corr + jnp.sum(p, axis=1, keepdims=True)
        return p.astype(BF16), corr

    def chunk(c, _):
        g = b * n_chunks + c
        slot = g % n_slots
        for cp in copies(g, slot):
            cp.wait()

        @pl.when(g + (n_slots - 1) < total)
        def _():
            start_all(g + (n_slots - 1), (g + (n_slots - 1)) % n_slots)

        s_a = []
        for h in range(n_heads):
            k_t = jnp.concatenate([kbuf[slot, jj, h].astype(BF16) for jj in range(gpc)], axis=1)
            s_a.append(_mm(q_a[h], k_t))
        ck = jnp.concatenate([cbuf[slot, jj].astype(BF16) for jj in range(gpc)], axis=0)
        kp_t = jnp.concatenate([pbuf[slot, jj].astype(BF16) for jj in range(gpc)], axis=1)
        s_b = _nt(q_lat, ck) + _mm(q_pe, kp_t)

        lf = lfbuf[slot]
        rows = gpc * n_heads
        hi, mid, lo = _split3(lf.reshape(rows, page))
        cs = _mm(jnp.concatenate([hi, mid, lo], axis=0).astype(BF16), tri_ref[...])
        cs = cs[:rows] + cs[rows:2 * rows] + cs[2 * rows:]
        excl = cs[:, :page].reshape(gpc, n_heads, page) - lf
        tot = cs[:, page:].reshape(gpc, n_heads, page)
        run = suf_ref[...]
        bias = [None] * gpc
        for jj in reversed(range(gpc)):
            bias[jj] = excl[jj] + run
            run = run + tot[jj]
        suf_ref[...] = run

        p_a, corr_a = [], []
        for h in range(n_heads):
            r_h = jnp.concatenate([jnp.broadcast_to(bias[jj][h:h + 1, :], (t_new, page)) for jj in range(gpc)], axis=1)
            p, corr = online(s_a[h] + cn[:, h:h + 1] + r_h, m_a, l_a, h)
            p_a.append(p)
            corr_a.append(corr)
        p_b, corr_b = online(s_b, m_b, l_b, ...)

        for h in range(n_heads):
            v_t = jnp.concatenate([vbuf[slot, jj, h].astype(BF16) for jj in range(gpc)], axis=1)
            acc_a[h] = acc_a[h] * corr_a[h] + _nt(p_a[h], v_t)
        acc_b[...] = acc_b[...] * corr_b + _mm(p_b, ck)
        return 0

    lax.fori_loop(0, n_chunks, chunk, 0)

    for h in range(n_heads):
        oa_ref[:, h * hd:(h + 1) * hd] = (acc_a[h] / l_a[h]).astype(oa_ref.dtype)
    olat_ref[...] = (acc_b[...] / l_b[...]).reshape(olat_ref.shape).astype(olat_ref.dtype)


def _sample_attention(page_table, qf, k32, v32, cum, qcat, ckv32, kpe32, kc, vc, lft, cc, pc, *,
                      t_new, page, pages_per_chunk):
    n_heads, ms, _ = qf.shape
    fw = k32.shape[1]
    hd = fw // n_heads
    rope = kpe32.shape[1]
    db = ms // t_new
    gpc = pages_per_chunk
    ns = SAMPLE_SLOTS
    assert db * (page_table.shape[1] // gpc) >= ns
    any_spec = pl.BlockSpec(memory_space=pl.ANY)
    row = lambda w: pl.BlockSpec((t_new, w), lambda b, pt: (b, 0))
    head = lambda w: pl.BlockSpec((n_heads, t_new, w), lambda b, pt: (0, b, 0))
    grid_spec = pltpu.PrefetchScalarGridSpec(
        num_scalar_prefetch=1, grid=(db,),
        in_specs=[head(LANES), row(fw), row(fw), row(n_heads), head(2 * LANES), row(LANES), row(rope),
                  any_spec, any_spec, any_spec, any_spec, any_spec,
                  pl.BlockSpec((page, 2 * page), lambda b, pt: (0, 0), pipeline_mode=pl.Buffered(1))],
        out_specs=(row(fw), head(LANES)),
        scratch_shapes=[
            pltpu.VMEM((ns, gpc, n_heads, hd, page), F32), pltpu.VMEM((ns, gpc, n_heads, hd, page), F32),
            pltpu.VMEM((ns, gpc, n_heads, page), F32), pltpu.VMEM((ns, gpc, page, LANES), F32),
            pltpu.VMEM((ns, gpc, rope, page), F32), pltpu.SemaphoreType.DMA((ns, 5)),
            pltpu.VMEM((n_heads, t_new, 1), F32), pltpu.VMEM((n_heads, t_new, 1), F32),
            pltpu.VMEM((n_heads, t_new, hd), F32),
            pltpu.VMEM((n_heads * t_new, 1), F32), pltpu.VMEM((n_heads * t_new, 1), F32),
            pltpu.VMEM((n_heads * t_new, LANES), F32), pltpu.VMEM((n_heads, page), F32),
        ])
    r = np.arange(page)
    tri = jnp.asarray(np.concatenate([r[:, None] >= r[None, :], np.ones((page, page), bool)], axis=1), BF16)
    body = functools.partial(_sample_attn_kernel, pages_per_chunk=gpc, page=page, hd=hd, rope=rope)
    return pl.pallas_call(
        body, grid_spec=grid_spec,
        out_shape=(jax.ShapeDtypeStruct((ms, fw), F32), jax.ShapeDtypeStruct((n_heads, ms, LANES), F32)),
        compiler_params=pltpu.CompilerParams(dimension_semantics=("arbitrary",), vmem_limit_bytes=VMEM_LIMIT),
        name="sample_attn",
    )(page_table, qf, k32, v32, cum, qcat, ckv32, kpe32, kc, vc, lft, cc, pc, tri)


def _route(logits_t, bias_col):
    n_exp, tm = logits_t.shape
    epg = n_exp // N_GROUPS
    score = _sigmoid(logits_t)
    biased = score + bias_col
    b3 = biased.reshape(N_GROUPS, epg, tm)
    sub = lax.broadcasted_iota(jnp.int32, b3.shape, 1)
    m1 = jnp.max(b3, axis=1, keepdims=True)
    i1 = jnp.min(jnp.where(b3 == m1, sub, epg), axis=1, keepdims=True)
    m2 = jnp.max(jnp.where(sub == i1, -jnp.inf, b3), axis=1, keepdims=True)
    g = (m1 + m2).reshape(N_GROUPS, tm)
    gidx = lax.broadcasted_iota(jnp.int32, g.shape, 0)
    gmask = jnp.zeros(g.shape, F32)
    for _ in range(TOPK_GROUPS):
        mg = jnp.max(g, axis=0, keepdims=True)
        ig = jnp.min(jnp.where(g == mg, gidx, N_GROUPS), axis=0, keepdims=True)
        sel = gidx == ig
        gmask = jnp.where(sel, 1.0, gmask)
        g = jnp.where(sel, -jnp.inf, g)
    emask = jnp.concatenate([jnp.broadcast_to(gmask[k:k + 1, :], (epg, tm)) for k in range(N_GROUPS)], axis=0)
    v = jnp.where(emask > 0, biased, -jnp.inf)
    eidx = lax.broadcasted_iota(jnp.int32, v.shape, 0)
    chosen = jnp.zeros(v.shape, F32)
    for _ in range(TOP_K):
        mv = jnp.max(v, axis=0, keepdims=True)
        iv = jnp.min(jnp.where(v == mv, eidx, n_exp), axis=0, keepdims=True)
        sel = eidx == iv
        chosen = jnp.where(sel, 1.0, chosen)
        v = jnp.where(sel, -jnp.inf, v)
    w = chosen * score
    return w / jnp.sum(w, axis=0, keepdims=True) * ROUTED_SCALE


def _post_kernel(x_ref, oa_ref, olat_ref, sga_ref, sgb_ref, wuv_ref, wba_ref, wbb_ref, wout_ref, gffn_ref,
                 wrh_ref, wrl_ref, rbias_ref, wsg_ref, wsu_ref, wsd_ref, x3_ref, hn_ref, gates_ref):
    n_heads = olat_ref.shape[0]
    tm = x_ref.shape[0]
    branch_a = _mm(oa_ref[...].astype(BF16), wba_ref[...])
    ob = jnp.concatenate(
        [_mm(jnp.concatenate([olat_ref[2 * j].astype(BF16), olat_ref[2 * j + 1].astype(BF16)], axis=1), wuv_ref[j])
         for j in range(n_heads // 2)], axis=1).astype(BF16)
    branch_b = _mm(ob, wbb_ref[...])
    merged = sga_ref[...].astype(F32) * branch_a + sgb_ref[...].astype(F32) * branch_b
    x2 = x_ref[...] + _mm(merged.astype(BF16), wout_ref[...])
    hn = _rms(x2, gffn_ref[...])
    hb = hn.astype(BF16)
    hn_ref[...] = hb
    h_lo = (hn - hb.astype(F32)).astype(BF16)
    wrh = wrh_ref[...]
    logits_t = _nt(wrh, hb) + _nt(wrh, h_lo) + _nt(wrl_ref[...], hb)
    gates_t = _route(logits_t, rbias_ref[...])
    n_exp = gates_t.shape[0]
    gates_ref[...] = jnp.concatenate([gates_t, jnp.zeros((LANES - n_exp, tm), F32)], axis=0).T
    sg = _mm(hb, wsg_ref[...])
    su = _mm(hb, wsu_ref[...])
    act = (sg * _sigmoid(sg) * su).astype(BF16)
    x3_ref[...] = x2 + _mm(act, wsd_ref[...])


def _post(x2d, oa, olat, sga, sgb, qw, *, tm, m_total, row_offset, prev=None):
    m, d = x2d.shape
    n_heads = olat.shape[0]
    first = row_offset // tm
    row = lambda w: pl.BlockSpec((tm, w), lambda i: (i, 0))
    out_row = lambda w: pl.BlockSpec((tm, w), lambda i: (i + first, 0))
    names = ("w_uv_bd", "w_ba", "w_bb", "w_out", "g_ffn", "w_r_hi", "w_r_lo", "r_bias", "w_sg", "w_su", "w_sd")
    in_specs = [row(d), row(oa.shape[1]), pl.BlockSpec((n_heads, tm, LANES), lambda i: (0, i, 0)), row(d), row(d)]
    in_specs += [_const_spec(qw[k].shape) for k in names]
    args = [x2d, oa, olat, sga, sgb, *[qw[k] for k in names]]
    n_in = m // tm
    if prev is None:
        assert row_offset == 0
        grid = (m_total // tm,)
        clamp = lambda i: jnp.minimum(i, n_in - 1)
        in_specs[:5] = [pl.BlockSpec((tm, d), lambda i: (clamp(i), 0)), pl.BlockSpec((tm, oa.shape[1]), lambda i: (clamp(i), 0)),
                        pl.BlockSpec((n_heads, tm, LANES), lambda i: (0, clamp(i), 0)),
                        pl.BlockSpec((tm, d), lambda i: (clamp(i), 0)), pl.BlockSpec((tm, d), lambda i: (clamp(i), 0))]
        aliases = {}

        def body(*refs):
            i = pl.program_id(0)

            @pl.when(i < n_in)
            def _():
                _post_kernel(*refs)

            @pl.when(i >= n_in)
            def _():
                for out in refs[-3:]:
                    out[...] = jnp.zeros(out.shape, out.dtype)
    else:
        grid = (n_in,)
        aliases = {len(args) + k: k for k in range(len(prev))}
        in_specs += [pl.BlockSpec(memory_space=pl.ANY)] * len(prev)
        args += list(prev)
        body = lambda *refs: _post_kernel(*refs[:len(refs) - 2 * len(prev)], *refs[len(refs) - len(prev):])
    return pl.pallas_call(
        body,
        out_shape=(jax.ShapeDtypeStruct((m_total, d), F32), jax.ShapeDtypeStruct((m_total, d), BF16),
                   jax.ShapeDtypeStruct((m_total, LANES), F32)),
        grid=grid, in_specs=in_specs, out_specs=(out_row(d), out_row(d), out_row(LANES)),
        input_output_aliases=aliases,
        compiler_params=pltpu.CompilerParams(dimension_semantics=("arbitrary",), vmem_limit_bytes=VMEM_LIMIT),
        name="post",
    )(*args)


def _moe_kernel(hn_ref, gates_ref, x3_ref, wg_ref, wu_ref, wd_ref, gfin_ref, y_ref, y2_ref, *, n_split, tail_start):
    i = pl.program_id(0)
    e = pl.program_id(1)
    rows = hn_ref.shape[0] // n_split

    @pl.when(e == 0)
    def _():
        y_ref[...] = x3_ref[...]

    wg = wg_ref[0].astype(BF16)
    wu = wu_ref[0].astype(BF16)
    wd = wd_ref[0].astype(BF16)
    parts = []
    for r in range(n_split):
        hb = hn_ref[r * rows:(r + 1) * rows, :]
        parts.append((_mm(hb, wg), _mm(hb, wu)))
    for r, (hg, hu) in enumerate(parts):
        gates = gates_ref[r * rows:(r + 1) * rows, :]
        lane = lax.broadcasted_iota(jnp.int32, gates.shape, 1)
        gate = jnp.sum(jnp.where(lane == e, gates, 0.0), axis=1, keepdims=True)
        act = (hg * _sigmoid(hg) * hu * gate).astype(BF16)
        y_ref[r * rows:(r + 1) * rows, :] += _mm(act, wd)

    last = e == pl.num_programs(1) - 1

    @pl.when(last)
    def _():
        y_ref[...] = _rms(y_ref[...], gfin_ref[...])

    @pl.when(last & (i == pl.num_programs(0) - 1))
    def _():
        y2_ref[...] = y_ref[tail_start:, :]


def _moe(hn, gates, x3, w_gate, w_up, w_down, g_final, *, tm, m_first):
    m, d = x3.shape
    n_exp, _, ff = w_gate.shape
    n_tiles = m // tm
    m_second = m - m_first
    tail_start = m_first - (n_tiles - 1) * tm
    assert n_tiles * tm == m and 0 <= tail_start and tail_start + m_second == tm and tail_start % SUBLANES == 0
    once = lambda w: pl.BlockSpec((tm, w), lambda i, e: (i, 0), pipeline_mode=pl.Buffered(1))
    return pl.pallas_call(
        functools.partial(_moe_kernel, n_split=MOE_ROW_SPLIT, tail_start=tail_start),
        out_shape=(jax.ShapeDtypeStruct((m_first, d), F32), jax.ShapeDtypeStruct((m_second, d), F32)),
        grid=(n_tiles, n_exp),
        in_specs=[once(d), once(LANES), once(d),
                  pl.BlockSpec((1, d, ff), lambda i, e: (e, 0, 0)), pl.BlockSpec((1, d, ff), lambda i, e: (e, 0, 0)),
                  pl.BlockSpec((1, ff, d), lambda i, e: (e, 0, 0)), _const_spec(g_final.shape)],
        out_specs=(pl.BlockSpec((tm, d), lambda i, e: (i, 0)), pl.BlockSpec((m_second, d), lambda i, e: (0, 0))),
        compiler_params=pltpu.CompilerParams(dimension_semantics=("arbitrary", "arbitrary"),
                                             vmem_limit_bytes=VMEM_LIMIT),
        name="moe",
    )(hn, gates, x3, w_gate, w_up, w_down, g_final)


def _prep_proj_weights(g_attn, w_in, b_forget, g_q_lat, w_uq, g_kv_lat, w_uk):
    q_rank, n_heads, qk = w_uq.shape
    kv_rank, _, nope = w_uk.shape
    rope = qk - nope
    half = rope // 2
    fw = (w_in.shape[1] - n_heads - q_rank - kv_rank - rope - 2 * w_in.shape[0]) // 3
    sizes = (fw, fw, fw, n_heads, q_rank, kv_rank, rope, w_in.shape[0], w_in.shape[0])
    qa, ka, va, fa, cq, ckv, kr, ga, gb = jnp.split(w_in, [int(v) for v in np.cumsum(sizes)[:-1]], axis=1)
    kr_sw = jnp.concatenate([-kr[:, half:], kr[:, :half]], axis=1)
    rep = LANES // rope
    misc = jnp.concatenate([fa, jnp.zeros((w_in.shape[0], LANES - n_heads), F32),
                            jnp.tile(kr, (1, rep)), jnp.tile(kr_sw, (1, rep))], axis=1)
    w_all = jnp.concatenate([qa, ka, va, cq, ckv, misc, ga, gb], axis=1).astype(BF16)
    assert w_all.shape[1] == _C_END, w_all.shape
    w_pe = w_uq[:, :, nope:]
    w_pe_sw = jnp.concatenate([-w_pe[..., half:], w_pe[..., :half]], axis=-1)
    w_uq_all = jnp.concatenate([w_uq[:, :, :nope].reshape(q_rank, -1), w_pe.reshape(q_rank, -1),
                                w_pe_sw.reshape(q_rank, -1)], axis=1).astype(BF16)
    wk = jnp.transpose(w_uk, (1, 2, 0))
    z = jnp.zeros_like(wk[0])
    w_uk_bd = jnp.stack([jnp.concatenate([jnp.concatenate([wk[2 * j], z], axis=1),
                                          jnp.concatenate([z, wk[2 * j + 1]], axis=1)], axis=0)
                         for j in range(n_heads // 2)]).astype(BF16)
    b_pad = jnp.concatenate([b_forget, jnp.zeros((LANES - n_heads,), F32)])[None, :]
    return dict(g_attn=g_attn[None, :], w_all=w_all, b_pad=b_pad, g_q=g_q_lat[None, :], w_uq_all=w_uq_all,
                g_kv=g_kv_lat[None, :], w_uk_bd=w_uk_bd, mla_scale=float(qk) ** -0.5)


def _prep_post_weights(w_uv, w_branch_a, w_branch_b, w_out, g_ffn, w_router, router_bias, w_sh_gate, w_sh_up,
                       w_sh_down):
    n_heads = w_uv.shape[1]
    wv = jnp.transpose(w_uv, (1, 0, 2))
    z = jnp.zeros_like(wv[0])
    w_uv_bd = jnp.stack([jnp.concatenate([jnp.concatenate([wv[2 * j], z], axis=1),
                                          jnp.concatenate([z, wv[2 * j + 1]], axis=1)], axis=0)
                         for j in range(n_heads // 2)]).astype(BF16)
    wr_t = w_router.T
    wr_hi = wr_t.astype(BF16)
    wr_lo = (wr_t - wr_hi.astype(F32)).astype(BF16)
    return dict(w_uv_bd=w_uv_bd, w_ba=w_branch_a.astype(BF16), w_bb=w_branch_b.astype(BF16),
                w_out=w_out.astype(BF16), g_ffn=g_ffn[None, :], w_r_hi=wr_hi, w_r_lo=wr_lo,
                r_bias=router_bias[:, None], w_sg=w_sh_gate.astype(BF16), w_su=w_sh_up.astype(BF16),
                w_sd=w_sh_down.astype(BF16))


def _rope_tables(pos, rope, width):
    half = rope // 2
    inv_freq = ROPE_BASE ** (-jnp.arange(half, dtype=F32) / half)
    ang = pos.astype(F32)[:, None] * inv_freq[None, :]
    cos = jnp.cos(ang)
    sin = jnp.sin(ang)
    reps = width // rope
    return jnp.tile(jnp.concatenate([cos, cos], axis=1), (1, reps)), jnp.tile(jnp.concatenate([sin, sin], axis=1), (1, reps))


PROJ_TM = 512
FOX_TQ = 512
MLA_TQ = 256
POST_TM = 512
MOE_TILES = 8
MOE_ROW_SPLIT = 4
PAGES_PER_CHUNK = 8
SAMPLE_SLOTS = 3


def kernel(x_prompt, x_sample, cache_fox_k, cache_fox_v, cache_fox_logf, cache_mla_ckv, cache_mla_kpe, page_table, g_attn, w_in, b_forget, g_q_lat, w_uq, g_kv_lat, w_uk, w_uv, w_branch_a, w_branch_b, w_out, g_ffn, w_router, router_bias, w_exp_gate, w_exp_up, w_exp_down, w_sh_gate, w_sh_up, w_sh_down, g_final):
    batch, seq, d = x_prompt.shape
    db, t_new, _ = x_sample.shape
    depth = g_attn.shape[0]
    _, n_pool, page, n_heads, hd = cache_fox_k.shape
    kv_rank = cache_mla_ckv.shape[-1]
    rope = cache_mla_kpe.shape[-1]
    past_len = page_table.shape[1] * page
    mp, ms = batch * seq, db * t_new

    cos_p, sin_p = _rope_tables(jnp.arange(seq, dtype=jnp.int32), rope, 2 * LANES)
    pos_s = past_len + (jnp.arange(ms, dtype=jnp.int32) % t_new)
    cos_s, sin_s = _rope_tables(pos_s, rope, 2 * LANES)

    xp = x_prompt.reshape(mp, d)
    xs = x_sample.reshape(ms, d)
    st_p, st_s = [], []
    for l in range(depth):
        pw = _prep_proj_weights(g_attn[l], w_in[l], b_forget[l], g_q_lat[l], w_uq[l], g_kv_lat[l], w_uk[l])
        qw = _prep_post_weights(w_uv[l], w_branch_a[l], w_branch_b[l], w_out[l], g_ffn[l], w_router[l],
                                router_bias[l], w_sh_gate[l], w_sh_up[l], w_sh_down[l])
        dims = dict(n_heads=n_heads, hd=hd, kv_rank=kv_rank, rope=rope)

        (k_p, v_p, logf_p, _, ckv_p, kpe_p, qf_p, kf_p, vb_p, qcat_p, kcat_p, sga_p, sgb_p) = _proj(
            xp, pw, cos_p, sin_p, tm=PROJ_TM, seq=seq, q_dtype=BF16, kv_transposed=True, **dims)
        oa_p = _fox_attention(qf_p, kf_p, vb_p, batch=batch, seq=seq, tq=FOX_TQ)
        olat_p = _mla_attention(qcat_p, kcat_p, batch=batch, seq=seq, tq=MLA_TQ)

        (k_s, v_s, logf_s, cum_s, ckv_s, kpe_s, qf_s, _, _, qcat_s, _, sga_s, sgb_s) = _proj(
            xs, pw, cos_s, sin_s, tm=PROJ_TM, seq=t_new, q_dtype=F32, kv_transposed=False, **dims)
        kc = jnp.transpose(cache_fox_k[l], (0, 2, 3, 1))
        vc = jnp.transpose(cache_fox_v[l], (0, 2, 3, 1))
        lft = jnp.transpose(cache_fox_logf[l], (0, 2, 1))
        cc = cache_mla_ckv[l]
        pc = jnp.transpose(cache_mla_kpe[l], (0, 2, 1))
        oa_s, olat_s = _sample_attention(page_table, qf_s, k_s, v_s, cum_s, qcat_s, ckv_s, kpe_s, kc, vc, lft, cc, pc,
                                         t_new=t_new, page=page, pages_per_chunk=PAGES_PER_CHUNK)

        part = _post(xp, oa_p, olat_p, sga_p, sgb_p, qw, tm=POST_TM, m_total=mp + ms, row_offset=0)
        x3, hn, gates = _post(xs, oa_s, olat_s, sga_s, sgb_s, qw, tm=POST_TM, m_total=mp + ms, row_offset=mp, prev=part)

        last = l == depth - 1
        g_out = g_final[None, :] if last else None
        assert last, "the final norm is fused into the last layer's MoE kernel; depth > 1 needs an un-normed variant"
        xp, xs = _moe(hn, gates, x3, w_exp_gate[l], w_exp_up[l], w_exp_down[l], g_out, tm=(mp + ms) // MOE_TILES,
                      m_first=mp)
        unt = lambda a: jnp.transpose(a.reshape(batch, n_heads, hd, seq), (0, 3, 1, 2))
        st_p.append((unt(k_p), unt(v_p),
                     logf_p.reshape(batch, seq, n_heads), ckv_p.reshape(batch, seq, kv_rank),
                     kpe_p.reshape(batch, seq, rope)))
        st_s.append((k_s.reshape(db, t_new, n_heads, hd), v_s.reshape(db, t_new, n_heads, hd),
                     logf_s.reshape(db, t_new, n_heads), ckv_s.reshape(db, t_new, kv_rank),
                     kpe_s.reshape(db, t_new, rope)))

    stack = lambda sts, k: jnp.stack([s[k] for s in sts])
    return (xp.reshape(batch, seq, d), xs.reshape(db, t_new, d),
            *[stack(st_p, k) for k in range(5)], *[stack(st_s, k) for k in range(5)])
```
